```python
import math
import jax, jax.numpy as jnp
from jax import lax
import numpy as np

D_MODEL = 1024
BATCH = 2
SEQ = 16384
DEPTH = 4

GRID_W = 64
CTX_LEN = 256
HEAD_DIM = 64
N_Q_HEADS = D_MODEL // HEAD_DIM
H_GLOBAL = N_Q_HEADS // 2
KV_GLOBAL = 2
G_GLOBAL = H_GLOBAL // KV_GLOBAL
H_LOCAL = N_Q_HEADS - H_GLOBAL
KV_LOCAL = 2
G_LOCAL = H_LOCAL // KV_LOCAL
MIX_WIDTH = (H_GLOBAL + H_LOCAL) * HEAD_DIM
WINDOW = 128
Q_BLOCK = 128
ROPE_THETA = 10000.0
N_EXPERTS = 16
N_GROUPS = 4
EXPERTS_PER_GROUP = N_EXPERTS // N_GROUPS
TOP_K = 2
D_EXPERT = D_MODEL // 2
NORM_EPS = 1e-6
ATTN_SCALE = 1.0 / math.sqrt(HEAD_DIM)
NEG_INF = -1e30
WIDTHS = (H_GLOBAL * HEAD_DIM, KV_GLOBAL * HEAD_DIM, KV_GLOBAL * HEAD_DIM,
          H_LOCAL * HEAD_DIM, KV_LOCAL * HEAD_DIM, KV_LOCAL * HEAD_DIM)
IN_WIDTH = sum(WIDTHS)
SPLIT_POINTS = tuple(int(v) for v in np.cumsum(WIDTHS)[:-1])

kernel_name = "hybrid_dit_gqa_window_sink_grouped_moe"


def _rms(x, g):
    xf = x.astype(jnp.float32)
    y = xf * lax.rsqrt(jnp.mean(xf * xf, axis=-1, keepdims=True) + NORM_EPS)
    return (y * g.astype(jnp.float32)).astype(x.dtype)


def _modulate(h, shift, scale):
    return h * (1.0 + scale) + shift


def _axial_rope_tables(n_tokens):
    rows = n_tokens // GRID_W
    row = jnp.repeat(jnp.arange(rows, dtype=jnp.int32), GRID_W).astype(jnp.float32)
    col = jnp.tile(jnp.arange(GRID_W, dtype=jnp.int32), rows).astype(jnp.float32)
    n_freq = HEAD_DIM // 4
    inv = ROPE_THETA ** (-jnp.arange(n_freq, dtype=jnp.float32) / n_freq)
    ang = jnp.stack([row[:, None] * inv, col[:, None] * inv], axis=1)
    return jnp.cos(ang), jnp.sin(ang)


def _apply_rope(x, cos, sin):
    n_freq = HEAD_DIM // 4
    shp = x.shape
    xr = x.astype(jnp.float32).reshape(shp[:-1] + (2, 2, n_freq))
    x1, x2 = xr[..., 0, :], xr[..., 1, :]
    bshape = (1, shp[1]) + (1,) * (x.ndim - 3) + (2, n_freq)
    cs, sn = cos.reshape(bshape), sin.reshape(bshape)
    out = jnp.stack([x1 * cs - x2 * sn, x2 * cs + x1 * sn], axis=-2)
    return out.reshape(shp).astype(x.dtype)


def _project(t, w_in):
    b, n = t.shape[0], t.shape[1]
    qg, kg, vg, ql, kl, vl = jnp.split(t @ w_in, SPLIT_POINTS, axis=-1)
    return (qg.reshape(b, n, KV_GLOBAL, G_GLOBAL, HEAD_DIM),
            kg.reshape(b, n, KV_GLOBAL, HEAD_DIM), vg.reshape(b, n, KV_GLOBAL, HEAD_DIM),
            ql.reshape(b, n, KV_LOCAL, G_LOCAL, HEAD_DIM),
            kl.reshape(b, n, KV_LOCAL, HEAD_DIM), vl.reshape(b, n, KV_LOCAL, HEAD_DIM))


def _attend(q, k, v, sink, mask):
    b, nq, kvh, g, d = q.shape
    n = k.shape[1]
    s = jnp.einsum('bqkgd,bnkd->bkgqn', q, k, preferred_element_type=jnp.float32) * ATTN_SCALE
    if mask is not None:
        s = jnp.where(mask, s, NEG_INF)
    if sink is not None:
        sk = jnp.broadcast_to(sink.astype(jnp.float32).reshape(1, kvh, g, 1, 1), s.shape[:-1] + (1,))
        s = jnp.concatenate([s, sk], axis=-1)
    p = jax.nn.softmax(s, axis=-1)[..., :n].astype(v.dtype)
    o = jnp.einsum('bkgqn,bnkd->bqkgd', p, v)
    return o.reshape(b, nq, kvh * g * d)


def _global_latent(q, k, v, kc, vc):
    b, s, kvh, g, d = q.shape
    nblk = s // Q_BLOCK
    k_all = jnp.concatenate([kc, k], axis=1)
    v_all = jnp.concatenate([vc, v], axis=1)
    qb = q.reshape(b, nblk, Q_BLOCK, kvh, g, d).transpose(1, 0, 2, 3, 4, 5)
    o = lax.map(lambda qi: _attend(qi, k_all, v_all, None, None), qb)
    return o.transpose(1, 0, 2, 3).reshape(b, s, kvh * g * d)


def _local_latent(q, k, v, kc, vc, sink):
    b, s, kvh, g, d = q.shape
    nblk = s // Q_BLOCK
    span = Q_BLOCK + 2 * WINDOW
    pad = ((0, 0), (WINDOW, WINDOW), (0, 0), (0, 0))
    kp, vp = jnp.pad(k, pad), jnp.pad(v, pad)
    qb = q.reshape(b, nblk, Q_BLOCK, kvh, g, d).transpose(1, 0, 2, 3, 4, 5)
    qi_idx = jnp.arange(Q_BLOCK, dtype=jnp.int32)[:, None]
    kj_idx = jnp.arange(span, dtype=jnp.int32)[None, :]
    band = jnp.abs(kj_idx - WINDOW - qi_idx) <= WINDOW
    ctx_ok = jnp.ones((Q_BLOCK, kc.shape[1]), dtype=bool)

    def block(args):
        qi, n = args
        start = n * Q_BLOCK
        kb = lax.dynamic_slice_in_dim(kp, start, span, axis=1)
        vb = lax.dynamic_slice_in_dim(vp, start, span, axis=1)
        key_pos = start - WINDOW + kj_idx
        valid = band & (key_pos >= 0) & (key_pos < s)
        mask = jnp.concatenate([valid, ctx_ok], axis=-1)
        return _attend(qi, jnp.concatenate([kb, kc], axis=1), jnp.concatenate([vb, vc], axis=1), sink, mask)

    o = lax.map(block, (qb, jnp.arange(nblk, dtype=jnp.int32)))
    return o.transpose(1, 0, 2, 3).reshape(b, s, kvh * g * d)


def _token_mixers(h, hc, w_in, qn_g, kn_g, qn_l, kn_l, sink, cos, sin, with_ctx_out):
    qg, kg, vg, ql, kl, vl = _project(h, w_in)
    qgc, kgc, vgc, qlc, klc, vlc = _project(hc, w_in)
    qg = _apply_rope(_rms(qg, qn_g), cos, sin)
    kg = _apply_rope(_rms(kg, kn_g), cos, sin)
    ql = _apply_rope(_rms(ql, qn_l), cos, sin)
    kl = _apply_rope(_rms(kl, kn_l), cos, sin)
    qgc, kgc = _rms(qgc, qn_g), _rms(kgc, kn_g)
    qlc, klc = _rms(qlc, qn_l), _rms(klc, kn_l)
    y = jnp.concatenate([_global_latent(qg, kg, vg, kgc, vgc),
                         _local_latent(ql, kl, vl, klc, vlc, sink)], axis=-1)
    if not with_ctx_out:
        return y, None
    yc = jnp.concatenate([_attend(qgc, kgc, vgc, None, None),
                          _attend(qlc, klc, vlc, sink, None)], axis=-1)
    return y, yc


def _grouped_moe(t, router_w, router_b, w_gate, w_up, w_down):
    n = t.shape[0]
    scores = jax.nn.sigmoid((t @ router_w).astype(jnp.float32))
    sel = scores + router_b.astype(jnp.float32)
    grp_score = lax.top_k(sel.reshape(n, N_GROUPS, EXPERTS_PER_GROUP), TOP_K)[0].sum(-1)
    g_idx = lax.top_k(grp_score, 1)[1][:, 0]
    in_group = jnp.repeat(jax.nn.one_hot(g_idx, N_GROUPS, dtype=jnp.float32), EXPERTS_PER_GROUP, axis=-1) > 0
    e_idx = lax.top_k(jnp.where(in_group, sel, -jnp.inf), TOP_K)[1]
    w = jnp.take_along_axis(scores, e_idx, axis=-1)
    w = w / jnp.sum(w, axis=-1, keepdims=True)
    gates = jnp.sum(jax.nn.one_hot(e_idx, N_EXPERTS, dtype=jnp.float32) * w[..., None], axis=1)
    y = jnp.zeros(t.shape, jnp.float32)
    for e in range(N_EXPERTS):
        he = jax.nn.silu(t @ w_gate[e]) * (t @ w_up[e])
        y = y + gates[:, e:e + 1] * (he @ w_down[e]).astype(jnp.float32)
    return y.astype(t.dtype)


def setup_inputs(seed: int = 0) -> dict:
    key = jax.random.key(seed)
    ks = jax.random.split(key, 20)
    L, D, E, F = DEPTH, D_MODEL, N_EXPERTS, D_EXPERT

    def nrm(k, shape, s):
        return jax.random.normal(k, shape, jnp.float32) * s

    return {
        "x": nrm(ks[0], (BATCH, SEQ, D), 1.0),
        "c": nrm(ks[1], (BATCH, D), 1.0),
        "ctx": nrm(ks[2], (BATCH, CTX_LEN, D), 1.0),
        "c_ctx": nrm(ks[3], (D,), 1.0),
        "ada_w": nrm(ks[4], (L, D, 6 * D), 0.5 * D ** -0.5),
        "ada_b": nrm(ks[5], (L, 6 * D), 0.02),
        "attn_norm_g": 1.0 + nrm(ks[6], (L, D), 0.02),
        "ffn_norm_g": 1.0 + nrm(ks[7], (L, D), 0.02),
        "w_in": nrm(ks[8], (L, D, IN_WIDTH), D ** -0.5),
        "qn_global": 1.0 + nrm(ks[9], (L, HEAD_DIM), 0.02),
        "kn_global": 1.0 + nrm(ks[10], (L, HEAD_DIM), 0.02),
        "qn_local": 1.0 + nrm(ks[11], (L, HEAD_DIM), 0.02),
        "kn_local": 1.0 + nrm(ks[12], (L, HEAD_DIM), 0.02),
        "sink_logit": nrm(ks[13], (L, H_LOCAL), 0.5),
        "w_out": nrm(ks[14], (L, MIX_WIDTH, D), MIX_WIDTH ** -0.5),
        "router_w": nrm(ks[15], (D, E), D ** -0.5),
        "router_b": nrm(ks[16], (E,), 0.01),
        "w_gate": nrm(ks[17], (L, E, D, F), D ** -0.5),
        "w_up": nrm(ks[18], (L, E, D, F), D ** -0.5),
        "w_down": nrm(ks[19], (L, E, F, D), F ** -0.5),
    }


def reference(x, c, ctx, c_ctx, ada_w, ada_b, attn_norm_g, ffn_norm_g, w_in, qn_global, kn_global,
              qn_local, kn_local, sink_logit, w_out, router_w, router_b, w_gate, w_up, w_down):
    b, s, d = x.shape
    n_ctx = ctx.shape[1]
    cos, sin = _axial_rope_tables(s)
    silu_c = jax.nn.silu(c)
    silu_cc = jax.nn.silu(c_ctx)
    xc = ctx
    for l in range(DEPTH):
        need_ctx = l < DEPTH - 1
        sh1, sc1, g1, sh2, sc2, g2 = [m[:, None, :] for m in jnp.split(silu_c @ ada_w[l] + ada_b[l], 6, axis=-1)]
        sh1c, sc1c, g1c, sh2c, sc2c, g2c = jnp.split(silu_cc @ ada_w[l] + ada_b[l], 6, axis=-1)
        h = _modulate(_rms(x, attn_norm_g[l]), sh1, sc1)
        hc = _modulate(_rms(xc, attn_norm_g[l]), sh1c, sc1c)
        y, yc = _token_mixers(h, hc, w_in[l], qn_global[l], kn_global[l], qn_local[l], kn_local[l],
                              sink_logit[l], cos, sin, need_ctx)
        x = x + g1 * (y @ w_out[l])
        h = _modulate(_rms(x, ffn_norm_g[l]), sh2, sc2)
        if need_ctx:
            xc = xc + g1c * (yc @ w_out[l])
            hc = _modulate(_rms(xc, ffn_norm_g[l]), sh2c, sc2c)
            tokens = jnp.concatenate([h.reshape(b * s, d), hc.reshape(b * n_ctx, d)], axis=0)
            yf = _grouped_moe(tokens, router_w, router_b, w_gate[l], w_up[l], w_down[l])
            x = x + g2 * yf[:b * s].reshape(b, s, d)
            xc = xc + g2c * yf[b * s:].reshape(b, n_ctx, d)
        else:
            yf = _grouped_moe(h.reshape(b * s, d), router_w, router_b, w_gate[l], w_up[l], w_down[l])
            x = x + g2 * yf.reshape(b, s, d)
    return x
```

```python
import functools
import math

import numpy as np
import jax
import jax.numpy as jnp
from jax import lax
from jax.experimental import pallas as pl
from jax.experimental.pallas import tpu as pltpu

F32 = jnp.float32
BF16 = jnp.bfloat16

HEAD_DIM = 64
GRID_W = 64
WINDOW = 128
N_HEADS_MIXER = 8
KV_HEADS = 2
GROUP = N_HEADS_MIXER // KV_HEADS
ROPE_THETA = 10000.0
N_EXPERTS = 16
N_GROUPS = 4
EXPERTS_PER_GROUP = N_EXPERTS // N_GROUPS
NORM_EPS = 1e-6
ATTN_SCALE = 1.0 / math.sqrt(HEAD_DIM)
NEG_INF = -1e30
N_FREQ = HEAD_DIM // 4

LANES = 128
TOKEN_TILE = 512
ONES_ROWS = 16
V_ROWS = HEAD_DIM + ONES_ROWS
VMEM_LIMIT = 48 * 1024 * 1024
MOD_ROWS = 16


def _dot(a, b):
    return jnp.dot(a, b, preferred_element_type=F32)


def _sigmoid(x):
    return 1.0 / (1.0 + jnp.exp(-x))


def _mod_kernel(c_ref, w_ref, b_ref, o_ref):
    c = c_ref[...]
    s = (c * _sigmoid(c)).astype(BF16)
    o_ref[0] = _dot(s, w_ref[0].astype(BF16)) + b_ref[0]


def _modulation(cvec, ada_w, ada_b):
    depth, d, d6 = ada_w.shape
    col = 1536
    assert d6 % col == 0
    return pl.pallas_call(
        _mod_kernel,
        grid=(depth, d6 // col),
        in_specs=[pl.BlockSpec((MOD_ROWS, d), lambda l, j: (0, 0)),
                  pl.BlockSpec((1, d, col), lambda l, j: (l, 0, j)),
                  pl.BlockSpec((1, 1, col), lambda l, j: (l, 0, j))],
        out_specs=pl.BlockSpec((1, MOD_ROWS, col), lambda l, j: (l, 0, j)),
        out_shape=jax.ShapeDtypeStruct((depth, MOD_ROWS, d6), F32),
        compiler_params=pltpu.CompilerParams(dimension_semantics=("arbitrary", "arbitrary"),
                                             vmem_limit_bytes=VMEM_LIMIT),
        name="modulation",
    )(cvec, ada_w, ada_b.reshape(depth, 1, d6))


def _rms_rows(x, gain):
    ms = jnp.mean(x * x, axis=-1, keepdims=True)
    return x * lax.rsqrt(ms + NORM_EPS) * gain


def _rope_t(t, cs_t):
    n = N_FREQ
    cr, sr, cc, sc = cs_t[0:n], cs_t[n:2 * n], cs_t[2 * n:3 * n], cs_t[3 * n:4 * n]
    x1r, x2r, x1c, x2c = t[0:n], t[n:2 * n], t[2 * n:3 * n], t[3 * n:4 * n]
    return jnp.concatenate([x1r * cr - x2r * sr, x2r * cr + x1r * sr,
                            x1c * cc - x2c * sc, x2c * cc + x1c * sc], axis=0)


def _q_heads_t(y_pair, gain_col, cs_t):
    t = y_pair.T
    outs = []
    for j in range(2):
        th = t[j * HEAD_DIM:(j + 1) * HEAD_DIM]
        ms = jnp.mean(th * th, axis=0, keepdims=True)
        thn = th * lax.rsqrt(ms + NORM_EPS) * gain_col
        outs.append(_rope_t(thn, cs_t) * ATTN_SCALE)
    return outs


def _k_heads(kc, block_ones, gain_row, cs):
    sq = kc * kc
    hi = sq.astype(BF16)
    lo = (sq - hi.astype(F32)).astype(BF16)
    ss = _dot(hi, block_ones) + _dot(lo, block_ones)
    kn = kc * lax.rsqrt(ss * (1.0 / HEAD_DIM) + NORM_EPS) * gain_row
    lane = lax.broadcasted_iota(jnp.int32, kn.shape, 1)
    first_half = (lane % (2 * N_FREQ)) < N_FREQ
    swapped = jnp.where(first_half, pltpu.roll(kn, LANES - N_FREQ, 1), pltpu.roll(kn, N_FREQ, 1))
    return kn * cs[:, :LANES] + swapped * cs[:, LANES:]


def _pre_kernel(x_ref, mod_ref, g_ref, w_ref, gq_ref, gk_ref, bones_ref, cs_ref, cst_ref,
                qgt_ref, kg_ref, vgt_ref, qlt_ref, kl_ref, vlt_ref):
    d = x_ref.shape[1]
    tm = x_ref.shape[0]
    shift = mod_ref[:, 0:d]
    scale = mod_ref[:, d:2 * d]
    h = _rms_rows(x_ref[...], g_ref[...]) * (1.0 + scale) + shift
    y = _dot(h.astype(BF16), w_ref[...])
    cs_t = cst_ref[...]
    cs = cs_ref[...]
    zeros = jnp.zeros((HEAD_DIM, tm), BF16)
    qw = N_HEADS_MIXER * HEAD_DIM
    kw = KV_HEADS * HEAD_DIM
    for mixer, (qt_ref, k_ref, vt_ref) in enumerate(((qgt_ref, kg_ref, vgt_ref), (qlt_ref, kl_ref, vlt_ref))):
        base = mixer * (qw + 2 * kw)
        gain_col = gq_ref[mixer]
        for c in range(N_HEADS_MIXER // 2):
            pair = _q_heads_t(y[:, base + c * LANES: base + (c + 1) * LANES], gain_col, cs_t)
            for j in range(2):
                hq = 2 * c + j
                data = pair[j].astype(BF16)
                if hq // GROUP == 0:
                    qt_ref[hq] = jnp.concatenate([data, zeros], axis=0)
                else:
                    qt_ref[hq] = jnp.concatenate([zeros, data], axis=0)
        kc = y[:, base + qw: base + qw + kw]
        k_ref[...] = _k_heads(kc, bones_ref[...], gk_ref[mixer], cs).astype(BF16)
        vc = y[:, base + qw + kw: base + qw + 2 * kw]
        vt = vc.T.astype(BF16)
        for kv in range(KV_HEADS):
            vt_ref[kv] = vt[kv * HEAD_DIM:(kv + 1) * HEAD_DIM]


def _pre_attention(x_all, mod, l, attn_g, w_in_b, gq_cols, gk_rows, block_ones, cs_nat, cs_t, dims):
    ntok, d = x_all.shape
    tm = TOKEN_TILE
    n_lat, per_b, nb = dims["n_lat_tiles"], dims["tiles_per_batch"], dims["batch"]
    n_tiles = ntok // tm

    def mod_idx(i):
        return (l, jnp.minimum(i // per_b, nb), 0, 0)

    def tab_idx(i):
        return jnp.where(i < n_lat, i % per_b, per_b)

    qt_shape = jax.ShapeDtypeStruct((N_HEADS_MIXER, LANES, ntok), BF16)
    k_shape = jax.ShapeDtypeStruct((ntok, LANES), BF16)
    vt_shape = jax.ShapeDtypeStruct((KV_HEADS, HEAD_DIM, ntok), BF16)
    qt_spec = pl.BlockSpec((N_HEADS_MIXER, LANES, tm), lambda i: (0, 0, i))
    k_spec = pl.BlockSpec((tm, LANES), lambda i: (i, 0))
    vt_spec = pl.BlockSpec((KV_HEADS, HEAD_DIM, tm), lambda i: (0, 0, i))
    return pl.pallas_call(
        _pre_kernel,
        grid=(n_tiles,),
        in_specs=[pl.BlockSpec((tm, d), lambda i: (i, 0)),
                  pl.BlockSpec((None, None, 1, mod.shape[-1]), mod_idx),
                  pl.BlockSpec((1, d), lambda i: (0, 0)),
                  pl.BlockSpec(w_in_b.shape, lambda i: (0, 0)),
                  pl.BlockSpec(gq_cols.shape, lambda i: (0, 0, 0)),
                  pl.BlockSpec(gk_rows.shape, lambda i: (0, 0, 0)),
                  pl.BlockSpec(block_ones.shape, lambda i: (0, 0)),
                  pl.BlockSpec((tm, 2 * LANES), lambda i: (tab_idx(i), 0)),
                  pl.BlockSpec((HEAD_DIM, tm), lambda i: (0, tab_idx(i)))],
        out_specs=[qt_spec, k_spec, vt_spec, qt_spec, k_spec, vt_spec],
        out_shape=[qt_shape, k_shape, vt_shape, qt_shape, k_shape, vt_shape],
        compiler_params=pltpu.CompilerParams(dimension_semantics=("arbitrary",), vmem_limit_bytes=VMEM_LIMIT),
        name="pre_attention",
    )(x_all, mod, attn_g, w_in_b, gq_cols, gk_rows, block_ones, cs_nat, cs_t)


def _with_ones(vt):
    return jnp.concatenate([vt, jnp.ones((ONES_ROWS, vt.shape[1]), BF16)], axis=0)


def _flash_kernel(q_ref, k_ref, v_ref, kc_ref, vc_ref, o_ref, m_scr, acc_scr):
    ki = pl.program_id(3)

    @pl.when(ki == 0)
    def _():
        vc1 = _with_ones(vc_ref[...])
        kc = kc_ref[...]
        for g in range(GROUP):
            st = _dot(kc, q_ref[g])
            m = jnp.max(st, axis=0, keepdims=True)
            p = jnp.exp(st - m).astype(BF16)
            acc_scr[g] = _dot(vc1, p)
            m_scr[g] = m

    v1 = _with_ones(v_ref[...])
    k = k_ref[...]
    for g in range(GROUP):
        st = _dot(k, q_ref[g])
        m_old = m_scr[g]
        m_new = jnp.maximum(m_old, jnp.max(st, axis=0, keepdims=True))
        alpha = jnp.exp(m_old - m_new)
        p = jnp.exp(st - m_new).astype(BF16)
        acc_scr[g] = alpha * acc_scr[g] + _dot(v1, p)
        m_scr[g] = m_new

    @pl.when(ki == pl.num_programs(3) - 1)
    def _():
        for g in range(GROUP):
            a = acc_scr[g]
            o_ref[g * HEAD_DIM:(g + 1) * HEAD_DIM, :] = (a[0:HEAD_DIM] / a[HEAD_DIM:HEAD_DIM + 1]).astype(BF16)


def _global_attention(qt, k, vt, dims, tq, tk):
    nb, s, n_ctx = dims["batch"], dims["seq"], dims["n_ctx"]
    nq, nk = s // tq, s // tk
    ctx_blk0 = (nb * s) // n_ctx
    return pl.pallas_call(
        _flash_kernel,
        grid=(nb, KV_HEADS, nq, nk),
        in_specs=[pl.BlockSpec((GROUP, LANES, tq), lambda b, kv, qi, ki: (kv, 0, b * nq + qi)),
                  pl.BlockSpec((tk, LANES), lambda b, kv, qi, ki: (b * nk + ki, 0)),
                  pl.BlockSpec((None, HEAD_DIM, tk), lambda b, kv, qi, ki: (kv, 0, b * nk + ki)),
                  pl.BlockSpec((n_ctx, LANES), lambda b, kv, qi, ki: (ctx_blk0 + b, 0)),
                  pl.BlockSpec((None, HEAD_DIM, n_ctx), lambda b, kv, qi, ki: (kv, 0, ctx_blk0 + b))],
        out_specs=pl.BlockSpec((GROUP * HEAD_DIM, tq), lambda b, kv, qi, ki: (kv, b * nq + qi)),
        out_shape=jax.ShapeDtypeStruct((N_HEADS_MIXER * HEAD_DIM, nb * s), BF16),
        scratch_shapes=[pltpu.VMEM((GROUP, 1, tq), F32), pltpu.VMEM((GROUP, V_ROWS, tq), F32)],
        compiler_params=pltpu.CompilerParams(
            dimension_semantics=("arbitrary", "arbitrary", "arbitrary", "arbitrary"), vmem_limit_bytes=VMEM_LIMIT),
        name="global_attention",
    )(qt, k, vt, k, vt)


def _single_pass_attend(k_all, v1_all, q_t, mask, sink):
    st = _dot(k_all, q_t)
    if mask is not None:
        st = jnp.where(mask, st, NEG_INF)
    m = jnp.max(st, axis=0, keepdims=True)
    if sink is not None:
        m = jnp.maximum(m, sink)
    p = jnp.exp(st - m).astype(BF16)
    a = _dot(v1_all, p)
    den = a[HEAD_DIM:HEAD_DIM + 1]
    if sink is not None:
        den = den + jnp.exp(sink - m)
    return a[0:HEAD_DIM] / den


def _local_kernel(sink_ref, q_ref, kp_ref, km_ref, kn_ref, kc_ref, vp_ref, vm_ref, vn_ref, vc_ref, o_ref):
    kv = pl.program_id(1)
    qi = pl.program_id(2)
    nq = pl.num_programs(2)
    tq = q_ref.shape[2]
    n_ctx = kc_ref.shape[0]
    span = tq + 2 * WINDOW
    k_all = jnp.concatenate([kp_ref[...], km_ref[...], kn_ref[...], kc_ref[...]], axis=0)
    v1_all = _with_ones(jnp.concatenate([vp_ref[...], vm_ref[...], vn_ref[...], vc_ref[...]], axis=1))
    shape = (span + n_ctx, tq)
    j = lax.broadcasted_iota(jnp.int32, shape, 0)
    i = lax.broadcasted_iota(jnp.int32, shape, 1)
    band = jnp.abs(j - WINDOW - i) <= WINDOW
    qi_v = jnp.full(shape, qi, jnp.int32)
    in_seq = ((j >= WINDOW) | (qi_v > 0)) & ((j < WINDOW + tq) | (qi_v < nq - 1))
    mask = (band & in_seq & (j < span)) | (j >= span)
    for g in range(GROUP):
        sink = jnp.full((1, tq), sink_ref[kv * GROUP + g], F32)
        o = _single_pass_attend(k_all, v1_all, q_ref[g], mask, sink)
        o_ref[g * HEAD_DIM:(g + 1) * HEAD_DIM, :] = o.astype(BF16)


def _local_attention(sink, qt, k, vt, dims, tq):
    nb, s, n_ctx = dims["batch"], dims["seq"], dims["n_ctx"]
    nq = s // tq
    w = WINDOW
    per_b_w = s // w
    r = tq // w
    ctx_blk0 = (nb * s) // n_ctx

    def prev_idx(b, qi):
        return b * per_b_w + jnp.maximum(qi * r - 1, 0)

    def next_idx(b, qi):
        return b * per_b_w + jnp.minimum(qi * r + r, per_b_w - 1)

    return pl.pallas_call(
        _local_kernel,
        grid=(nb, KV_HEADS, nq),
        in_specs=[pl.BlockSpec(memory_space=pltpu.SMEM),
                  pl.BlockSpec((GROUP, LANES, tq), lambda b, kv, qi: (kv, 0, b * nq + qi)),
                  pl.BlockSpec((w, LANES), lambda b, kv, qi: (prev_idx(b, qi), 0)),
                  pl.BlockSpec((tq, LANES), lambda b, kv, qi: (b * nq + qi, 0)),
                  pl.BlockSpec((w, LANES), lambda b, kv, qi: (next_idx(b, qi), 0)),
                  pl.BlockSpec((n_ctx, LANES), lambda b, kv, qi: (ctx_blk0 + b, 0)),
                  pl.BlockSpec((None, HEAD_DIM, w), lambda b, kv, qi: (kv, 0, prev_idx(b, qi))),
                  pl.BlockSpec((None, HEAD_DIM, tq), lambda b, kv, qi: (kv, 0, b * nq + qi)),
                  pl.BlockSpec((None, HEAD_DIM, w), lambda b, kv, qi: (kv, 0, next_idx(b, qi))),
                  pl.BlockSpec((None, HEAD_DIM, n_ctx), lambda b, kv, qi: (kv, 0, ctx_blk0 + b))],
        out_specs=pl.BlockSpec((GROUP * HEAD_DIM, tq), lambda b, kv, qi: (kv, b * nq + qi)),
        out_shape=jax.ShapeDtypeStruct((N_HEADS_MIXER * HEAD_DIM, nb * s), BF16),
        compiler_params=pltpu.CompilerParams(
            dimension_semantics=("arbitrary", "arbitrary", "arbitrary"), vmem_limit_bytes=VMEM_LIMIT),
        name="local_attention",
    )(sink, qt, k, k, k, k, vt, vt, vt, vt)


def _ctx_kernel(sink_ref, qg_ref, kg_ref, vg_ref, ql_ref, kl_ref, vl_ref, o_ref):
    tq = qg_ref.shape[2]
    for mixer, (q_ref, k_ref, v_ref) in enumerate(((qg_ref, kg_ref, vg_ref), (ql_ref, kl_ref, vl_ref))):
        k_all = k_ref[...]
        for kv in range(KV_HEADS):
            v1 = _with_ones(v_ref[kv])
            for g in range(GROUP):
                hq = kv * GROUP + g
                sink = jnp.full((1, tq), sink_ref[hq], F32) if mixer == 1 else None
                o = _single_pass_attend(k_all, v1, q_ref[hq], None, sink)
                row = (mixer * N_HEADS_MIXER + hq) * HEAD_DIM
                o_ref[row:row + HEAD_DIM, :] = o.astype(BF16)


def _ctx_attention(sink, qgt, kg, vgt, qlt, kl, vlt, dims):
    nb, s, n_ctx = dims["batch"], dims["seq"], dims["n_ctx"]
    ctx_blk0 = (nb * s) // n_ctx
    q_spec = pl.BlockSpec((N_HEADS_MIXER, LANES, n_ctx), lambda b: (0, 0, ctx_blk0 + b))
    k_spec = pl.BlockSpec((n_ctx, LANES), lambda b: (ctx_blk0 + b, 0))
    v_spec = pl.BlockSpec((KV_HEADS, HEAD_DIM, n_ctx), lambda b: (0, 0, ctx_blk0 + b))
    rows = 2 * N_HEADS_MIXER * HEAD_DIM
    return pl.pallas_call(
        _ctx_kernel,
        grid=(nb,),
        in_specs=[pl.BlockSpec(memory_space=pltpu.SMEM), q_spec, k_spec, v_spec, q_spec, k_spec, v_spec],
        out_specs=pl.BlockSpec((rows, n_ctx), lambda b: (0, b)),
        out_shape=jax.ShapeDtypeStruct((rows, nb * n_ctx), BF16),
        compiler_params=pltpu.CompilerParams(dimension_semantics=("arbitrary",), vmem_limit_bytes=VMEM_LIMIT),
        name="ctx_attention",
    )(sink, qgt, kg, vgt, qlt, kl, vlt)


def _route(sel, scores):
    n = EXPERTS_PER_GROUP
    group_score = []
    for g in range(N_GROUPS):
        r = sel[g * n:(g + 1) * n]
        best = None
        for a in range(n):
            for b in range(a + 1, n):
                pair = r[a] + r[b]
                best = pair if best is None else jnp.maximum(best, pair)
        group_score.append(best)
    gates = [None] * N_EXPERTS
    picked = []
    for g in range(N_GROUPS):
        chosen = None
        for h in range(N_GROUPS):
            if h == g:
                continue
            c = (group_score[g] > group_score[h]) if h < g else (group_score[g] >= group_score[h])
            chosen = c if chosen is None else (chosen & c)
        for a in range(n):
            ea = g * n + a
            rank = jnp.zeros_like(sel[ea])
            for b in range(n):
                if b == a:
                    continue
                eb = g * n + b
                ahead = (sel[eb] >= sel[ea]) if b < a else (sel[eb] > sel[ea])
                rank = rank + jnp.where(ahead, 1.0, 0.0)
            picked.append(chosen & (rank < 2.0))
    den = None
    for e in range(N_EXPERTS):
        w = jnp.where(picked[e], scores[e], 0.0)
        gates[e] = w
        den = w if den is None else den + w
    return [w / den for w in gates]


def _post_kernel(yg_ref, yl_ref, yc_ref, x_ref, mod_ref, wout_ref, g_ref, rwt_ref, rb_ref,
                 x1_ref, h2_ref, gates_ref, *, n_lat):
    i = pl.program_id(0)
    d = x_ref.shape[1]
    tm = x_ref.shape[0]
    yt_lat = jnp.concatenate([yg_ref[...], yl_ref[...]], axis=0)
    is_lat = jnp.full(yt_lat.shape, i, jnp.int32) < n_lat
    yt = jnp.where(is_lat, yt_lat, yc_ref[...])
    a = lax.dot_general(yt, wout_ref[...], (((0,), (0,)), ((), ())), preferred_element_type=F32)
    gate1 = mod_ref[:, 2 * d:3 * d]
    shift2 = mod_ref[:, 3 * d:4 * d]
    scale2 = mod_ref[:, 4 * d:5 * d]
    x1 = x_ref[...] + gate1 * a
    x1_ref[...] = x1
    h2 = (_rms_rows(x1, g_ref[...]) * (1.0 + scale2) + shift2).astype(BF16)
    h2_ref[...] = h2
    logits_t = lax.dot_general(rwt_ref[...], h2, (((1,), (1,)), ((), ())), preferred_element_type=F32)
    scores_t = _sigmoid(logits_t)
    sel_t = scores_t + rb_ref[...]
    gates = _route([sel_t[e:e + 1] for e in range(N_EXPERTS)], [scores_t[e:e + 1] for e in range(N_EXPERTS)])
    gates_t = jnp.concatenate(gates + [jnp.zeros((LANES - N_EXPERTS, tm), F32)], axis=0)
    gates_ref[...] = gates_t.T


def _post_attention(ygt, ylt, yct, x_all, mod, l, w_out_b, ffn_g, rw_t, rb_col, dims):
    ntok, d = x_all.shape
    tm = TOKEN_TILE
    n_lat, per_b, nb = dims["n_lat_tiles"], dims["tiles_per_batch"], dims["batch"]
    rows = ygt.shape[0]

    def mod_idx(i):
        return (l, jnp.minimum(i // per_b, nb), 0, 0)

    def lat_idx(i):
        return (0, jnp.minimum(i, n_lat - 1))

    return pl.pallas_call(
        functools.partial(_post_kernel, n_lat=n_lat),
        grid=(ntok // tm,),
        in_specs=[pl.BlockSpec((rows, tm), lat_idx),
                  pl.BlockSpec((rows, tm), lat_idx),
                  pl.BlockSpec((2 * rows, tm), lambda i: (0, 0)),
                  pl.BlockSpec((tm, d), lambda i: (i, 0)),
                  pl.BlockSpec((None, None, 1, mod.shape[-1]), mod_idx),
                  pl.BlockSpec(w_out_b.shape, lambda i: (0, 0)),
                  pl.BlockSpec((1, d), lambda i: (0, 0)),
                  pl.BlockSpec(rw_t.shape, lambda i: (0, 0)),
                  pl.BlockSpec(rb_col.shape, lambda i: (0, 0))],
        out_specs=[pl.BlockSpec((tm, d), lambda i: (i, 0)),
                   pl.BlockSpec((tm, d), lambda i: (i, 0)),
                   pl.BlockSpec((tm, LANES), lambda i: (i, 0))],
        out_shape=[jax.ShapeDtypeStruct((ntok, d), F32),
                   jax.ShapeDtypeStruct((ntok, d), BF16),
                   jax.ShapeDtypeStruct((ntok, LANES), F32)],
        compiler_params=pltpu.CompilerParams(dimension_semantics=("arbitrary",), vmem_limit_bytes=VMEM_LIMIT),
        name="post_attention",
    )(ygt, ylt, yct, x_all, mod, w_out_b, ffn_g, rw_t, rb_col)


def _moe_kernel(h_ref, gates_ref, wg_ref, wu_ref, wd_ref, x1_ref, mod_ref, o_ref, acc_scr):
    e = pl.program_id(1)
    d = x1_ref.shape[1]

    @pl.when(e == 0)
    def _():
        acc_scr[...] = jnp.zeros_like(acc_scr)

    h = h_ref[...]
    a = _dot(h, wg_ref[0, 0])
    u = _dot(h, wu_ref[0, 0])
    gates = gates_ref[...]
    lane = lax.broadcasted_iota(jnp.int32, gates.shape, 1)
    gcol = jnp.sum(jnp.where(lane == e, gates, 0.0), axis=1, keepdims=True)
    he = (a * _sigmoid(a)) * u * gcol
    acc_scr[...] += _dot(he.astype(BF16), wd_ref[0, 0])

    @pl.when(e == pl.num_programs(1) - 1)
    def _():
        gate2 = mod_ref[:, 5 * d:6 * d]
        o_ref[...] = x1_ref[...] + gate2 * acc_scr[...]


def _moe(h2, gates, wg_b, wu_b, wd_b, x1, mod, l, dims):
    ntok, d = x1.shape
    tm = TOKEN_TILE
    f = wg_b.shape[-1]
    per_b, nb = dims["tiles_per_batch"], dims["batch"]

    def mod_idx(i, e):
        return (l, jnp.minimum(i // per_b, nb), 0, 0)

    return pl.pallas_call(
        _moe_kernel,
        grid=(ntok // tm, N_EXPERTS),
        in_specs=[pl.BlockSpec((tm, d), lambda i, e: (i, 0)),
                  pl.BlockSpec((tm, LANES), lambda i, e: (i, 0)),
                  pl.BlockSpec((1, 1, d, f), lambda i, e: (l, e, 0, 0)),
                  pl.BlockSpec((1, 1, d, f), lambda i, e: (l, e, 0, 0)),
                  pl.BlockSpec((1, 1, f, d), lambda i, e: (l, e, 0, 0)),
                  pl.BlockSpec((tm, d), lambda i, e: (i, 0)),
                  pl.BlockSpec((None, None, 1, mod.shape[-1]), mod_idx)],
        out_specs=pl.BlockSpec((tm, d), lambda i, e: (i, 0)),
        out_shape=jax.ShapeDtypeStruct((ntok, d), F32),
        scratch_shapes=[pltpu.VMEM((tm, d), F32)],
        compiler_params=pltpu.CompilerParams(dimension_semantics=("arbitrary", "arbitrary"),
                                             vmem_limit_bytes=VMEM_LIMIT),
        name="moe_dense",
    )(h2, gates, wg_b, wu_b, wd_b, x1, mod)


def _rope_tables(s, pad):
    rows = s // GRID_W
    row = jnp.repeat(jnp.arange(rows, dtype=jnp.int32), GRID_W).astype(F32)
    col = jnp.tile(jnp.arange(GRID_W, dtype=jnp.int32), rows).astype(F32)
    inv = ROPE_THETA ** (-jnp.arange(N_FREQ, dtype=F32) / N_FREQ)
    ang_r, ang_c = row[:, None] * inv, col[:, None] * inv
    cr, sr, cc, sc = jnp.cos(ang_r), jnp.sin(ang_r), jnp.cos(ang_c), jnp.sin(ang_c)
    one, zero = jnp.ones((pad, N_FREQ), F32), jnp.zeros((pad, N_FREQ), F32)
    cs_t = jnp.concatenate([jnp.concatenate([cr, sr, cc, sc], axis=1),
                            jnp.concatenate([one, zero, one, zero], axis=1)], axis=0).T
    cpat = jnp.concatenate([cr, cr, cc, cc], axis=1)
    spat = jnp.concatenate([-sr, sr, -sc, sc], axis=1)
    ident = jnp.concatenate([jnp.ones((pad, 2 * HEAD_DIM), F32), jnp.zeros((pad, 2 * HEAD_DIM), F32)], axis=1)
    cs_nat = jnp.concatenate([jnp.concatenate([cpat, cpat, spat, spat], axis=1), ident], axis=0)
    return cs_nat, cs_t


def kernel(x, c, ctx, c_ctx, ada_w, ada_b, attn_norm_g, ffn_norm_g, w_in, qn_global, kn_global, qn_local, kn_local,
           sink_logit, w_out, router_w, router_b, w_gate, w_up, w_down):
    nb, s, d = x.shape
    n_ctx = ctx.shape[1]
    depth = ada_w.shape[0]
    tm = TOKEN_TILE
    assert s % tm == 0 and nb * n_ctx == tm and s % GRID_W == 0 and nb + 1 <= MOD_ROWS
    dims = dict(batch=nb, seq=s, n_ctx=n_ctx, tiles_per_batch=s // tm, n_lat_tiles=nb * s // tm)

    cvec = jnp.concatenate([c, c_ctx[None, :], jnp.zeros((MOD_ROWS - nb - 1, d), F32)], axis=0)
    mod = _modulation(cvec, ada_w, ada_b)[:, :nb + 1].reshape(depth, nb + 1, 1, 6 * d)

    cs_nat, cs_t = _rope_tables(s, tm)
    head_of_lane = np.arange(LANES) // HEAD_DIM
    block_ones = jnp.asarray(head_of_lane[:, None] == head_of_lane[None, :], BF16)
    w_in_b = w_in.astype(BF16)
    w_out_b = w_out.astype(BF16)
    wg_b, wu_b, wd_b = w_gate.astype(BF16), w_up.astype(BF16), w_down.astype(BF16)
    rw_t = router_w.T.astype(BF16)
    rb_col = router_b.reshape(N_EXPERTS, 1).astype(F32)

    x_all = jnp.concatenate([x.reshape(nb * s, d), ctx.reshape(nb * n_ctx, d)], axis=0)
    tq_g, tk_g = min(512, s), min(1024, s)
    tq_l = 256
    for l in range(depth):
        gq_cols = jnp.stack([qn_global[l], qn_local[l]]).reshape(2, HEAD_DIM, 1)
        gk_rows = jnp.stack([jnp.tile(kn_global[l], KV_HEADS), jnp.tile(kn_local[l], KV_HEADS)]).reshape(2, 1, LANES)
        qgt, kg, vgt, qlt, kl, vlt = _pre_attention(
            x_all, mod, l, attn_norm_g[l].reshape(1, d), w_in_b[l], gq_cols, gk_rows, block_ones, cs_nat, cs_t, dims)
        ygt = _global_attention(qgt, kg, vgt, dims, tq_g, tk_g)
        ylt = _local_attention(sink_logit[l], qlt, kl, vlt, dims, tq_l)
        yct = _ctx_attention(sink_logit[l], qgt, kg, vgt, qlt, kl, vlt, dims)
        x1, h2, gates = _post_attention(ygt, ylt, yct, x_all, mod, l, w_out_b[l], ffn_norm_g[l].reshape(1, d),
                                        rw_t, rb_col, dims)
        x_all = _moe(h2, gates, wg_b, wu_b, wd_b, x1, mod, l, dims)
    return x_all[:nb * s].reshape(nb, s, d)
```

```python
import functools
import math

import numpy as np
import jax
import jax.numpy as jnp
from jax import lax
from jax.experimental import pallas as pl
from jax.experimental.pallas import tpu as pltpu

F32 = jnp.float32
BF16 = jnp.bfloat16

HEAD_DIM = 64
GRID_W = 64
WINDOW = 128
N_HEADS_MIXER = 8
KV_HEADS = 2
GROUP = N_HEADS_MIXER // KV_HEADS
ROPE_THETA = 10000.0
N_EXPERTS = 16
N_GROUPS = 4
EXPERTS_PER_GROUP = N_EXPERTS // N_GROUPS
NORM_EPS = 1e-6
ATTN_SCALE = 1.0 / math.sqrt(HEAD_DIM)
LOG2_E = 1.4426950408889634
Q_SCALES = (ATTN_SCALE * LOG2_E, ATTN_SCALE)
NEG_INF = -1e30
N_FREQ = HEAD_DIM // 4

LANES = 128
TOKEN_TILE = 512
DIRTY_LOG2 = 60.0
ONES_ROWS = 16
V_ROWS = HEAD_DIM + ONES_ROWS
VMEM_LIMIT = 48 * 1024 * 1024
MOD_ROWS = 16


def _dot(a, b):
    return jnp.dot(a, b, preferred_element_type=F32)


def _sigmoid(x):
    return 1.0 / (1.0 + jnp.exp(-x))


def _mod_kernel(c_ref, w_ref, b_ref, o_ref):
    c = c_ref[...]
    s = (c * _sigmoid(c)).astype(BF16)
    o_ref[0] = _dot(s, w_ref[0].astype(BF16)) + b_ref[0]


def _modulation(cvec, ada_w, ada_b):
    depth, d, d6 = ada_w.shape
    col = 1536
    assert d6 % col == 0
    return pl.pallas_call(
        _mod_kernel,
        grid=(depth, d6 // col),
        in_specs=[pl.BlockSpec((MOD_ROWS, d), lambda l, j: (0, 0)),
                  pl.BlockSpec((1, d, col), lambda l, j: (l, 0, j)),
                  pl.BlockSpec((1, 1, col), lambda l, j: (l, 0, j))],
        out_specs=pl.BlockSpec((1, MOD_ROWS, col), lambda l, j: (l, 0, j)),
        out_shape=jax.ShapeDtypeStruct((depth, MOD_ROWS, d6), F32),
        compiler_params=pltpu.CompilerParams(dimension_semantics=("arbitrary", "arbitrary"),
                                             vmem_limit_bytes=VMEM_LIMIT),
        name="modulation",
    )(cvec, ada_w, ada_b.reshape(depth, 1, d6))


def _rms_rows(x, gain):
    ms = jnp.mean(x * x, axis=-1, keepdims=True)
    return x * lax.rsqrt(ms + NORM_EPS) * gain


def _rope_t(t, cs_t):
    n = N_FREQ
    cr, sr, cc, sc = cs_t[0:n], cs_t[n:2 * n], cs_t[2 * n:3 * n], cs_t[3 * n:4 * n]
    x1r, x2r, x1c, x2c = t[0:n], t[n:2 * n], t[2 * n:3 * n], t[3 * n:4 * n]
    return jnp.concatenate([x1r * cr - x2r * sr, x2r * cr + x1r * sr,
                            x1c * cc - x2c * sc, x2c * cc + x1c * sc], axis=0)


def _q_heads_t(y_pair, gain_col, cs_t, q_scale):
    t = y_pair.T
    outs = []
    for j in range(2):
        th = t[j * HEAD_DIM:(j + 1) * HEAD_DIM]
        ms = jnp.mean(th * th, axis=0, keepdims=True)
        thn = th * lax.rsqrt(ms + NORM_EPS) * gain_col
        outs.append(_rope_t(thn, cs_t) * q_scale)
    return outs


def _k_heads(kc, block_ones, gain_row, cs):
    sq = kc * kc
    hi = sq.astype(BF16)
    lo = (sq - hi.astype(F32)).astype(BF16)
    ss = _dot(hi, block_ones) + _dot(lo, block_ones)
    kn = kc * lax.rsqrt(ss * (1.0 / HEAD_DIM) + NORM_EPS) * gain_row
    lane = lax.broadcasted_iota(jnp.int32, kn.shape, 1)
    first_half = (lane % (2 * N_FREQ)) < N_FREQ
    swapped = jnp.where(first_half, pltpu.roll(kn, LANES - N_FREQ, 1), pltpu.roll(kn, N_FREQ, 1))
    return kn * cs[:, :LANES] + swapped * cs[:, LANES:]


def _pre_kernel(x_ref, mod_ref, g_ref, w_ref, gq_ref, gk_ref, bones_ref, cs_ref, cst_ref,
                qgt_ref, kg_ref, vgt_ref, qlt_ref, kl_ref, vlt_ref):
    d = x_ref.shape[1]
    tm = x_ref.shape[0]
    shift = mod_ref[:, 0:d]
    scale = mod_ref[:, d:2 * d]
    h = _rms_rows(x_ref[...], g_ref[...]) * (1.0 + scale) + shift
    y = _dot(h.astype(BF16), w_ref[...])
    cs_t = cst_ref[...]
    cs = cs_ref[...]
    zeros = jnp.zeros((HEAD_DIM, tm), BF16)
    qw = N_HEADS_MIXER * HEAD_DIM
    kw = KV_HEADS * HEAD_DIM
    for mixer, (qt_ref, k_ref, vt_ref) in enumerate(((qgt_ref, kg_ref, vgt_ref), (qlt_ref, kl_ref, vlt_ref))):
        base = mixer * (qw + 2 * kw)
        gain_col = gq_ref[mixer]
        q_scale = Q_SCALES[mixer]
        for c in range(N_HEADS_MIXER // 2):
            pair = _q_heads_t(y[:, base + c * LANES: base + (c + 1) * LANES], gain_col, cs_t, q_scale)
            for j in range(2):
                qt_ref[2 * c + j] = jnp.concatenate([pair[j].astype(BF16), zeros], axis=0)
        kc = y[:, base + qw: base + qw + kw]
        kboth = _k_heads(kc, bones_ref[...], gk_ref[mixer], cs)
        lane = lax.broadcasted_iota(jnp.int32, kboth.shape, 1)
        ones_col = jnp.where(lane == HEAD_DIM, 1.0, 0.0)
        k_ref[0] = jnp.where(lane < HEAD_DIM, kboth, ones_col).astype(BF16)
        k_ref[1] = jnp.where(lane < HEAD_DIM, pltpu.roll(kboth, HEAD_DIM, 1), ones_col).astype(BF16)
        vc = y[:, base + qw + kw: base + qw + 2 * kw]
        vt = vc.T.astype(BF16)
        for kv in range(KV_HEADS):
            vt_ref[kv] = vt[kv * HEAD_DIM:(kv + 1) * HEAD_DIM]


def _pre_attention(x_all, mod, l, attn_g, w_in_b, gq_cols, gk_rows, block_ones, cs_nat, cs_t, dims):
    ntok, d = x_all.shape
    tm = TOKEN_TILE
    n_lat, per_b, nb = dims["n_lat_tiles"], dims["tiles_per_batch"], dims["batch"]
    n_tiles = ntok // tm

    def mod_idx(i):
        return (l, jnp.minimum(i // per_b, nb), 0, 0)

    def tab_idx(i):
        return jnp.where(i < n_lat, i % per_b, per_b)

    qt_shape = jax.ShapeDtypeStruct((N_HEADS_MIXER, LANES, ntok), BF16)
    k_shape = jax.ShapeDtypeStruct((KV_HEADS, ntok, LANES), BF16)
    vt_shape = jax.ShapeDtypeStruct((KV_HEADS, HEAD_DIM, ntok), BF16)
    qt_spec = pl.BlockSpec((N_HEADS_MIXER, LANES, tm), lambda i: (0, 0, i))
    k_spec = pl.BlockSpec((KV_HEADS, tm, LANES), lambda i: (0, i, 0))
    vt_spec = pl.BlockSpec((KV_HEADS, HEAD_DIM, tm), lambda i: (0, 0, i))
    return pl.pallas_call(
        _pre_kernel,
        grid=(n_tiles,),
        in_specs=[pl.BlockSpec((tm, d), lambda i: (i, 0)),
                  pl.BlockSpec((None, None, 1, mod.shape[-1]), mod_idx),
                  pl.BlockSpec((1, d), lambda i: (0, 0)),
                  pl.BlockSpec(w_in_b.shape, lambda i: (0, 0)),
                  pl.BlockSpec(gq_cols.shape, lambda i: (0, 0, 0)),
                  pl.BlockSpec(gk_rows.shape, lambda i: (0, 0, 0)),
                  pl.BlockSpec(block_ones.shape, lambda i: (0, 0)),
                  pl.BlockSpec((tm, 2 * LANES), lambda i: (tab_idx(i), 0)),
                  pl.BlockSpec((HEAD_DIM, tm), lambda i: (0, tab_idx(i)))],
        out_specs=[qt_spec, k_spec, vt_spec, qt_spec, k_spec, vt_spec],
        out_shape=[qt_shape, k_shape, vt_shape, qt_shape, k_shape, vt_shape],
        compiler_params=pltpu.CompilerParams(dimension_semantics=("arbitrary",), vmem_limit_bytes=VMEM_LIMIT),
        name="pre_attention",
    )(x_all, mod, attn_g, w_in_b, gq_cols, gk_rows, block_ones, cs_nat, cs_t)


def _with_ones(vt):
    return jnp.concatenate([vt, jnp.ones((ONES_ROWS, vt.shape[1]), BF16)], axis=0)


def _round_bf16(x):
    return x.astype(BF16).astype(F32)


def _flash_kernel(q_ref, k_ref, v_ref, kc_ref, vc_ref, o_ref, m_scr, acc_scr, new_scr, pmax_scr):
    ki = pl.program_id(3)
    tq = q_ref.shape[2]
    zpad = jnp.zeros((LANES - HEAD_DIM - ONES_ROWS, tq), BF16)

    def q_ext(g, m_row):
        neg = jnp.concatenate([-m_row, jnp.zeros((ONES_ROWS - 1, tq), F32)], axis=0).astype(BF16)
        return jnp.concatenate([q_ref[g, 0:HEAD_DIM], neg, zpad], axis=0)

    def two_pass_update(g, k, v1, first):
        m_old = jnp.zeros((1, tq), F32) if first else m_scr[g]
        st = _dot(k, q_ext(g, m_old))
        blk = jnp.max(st, axis=0, keepdims=True)
        delta = _round_bf16(m_old + (blk if first else jnp.maximum(blk, 0.0))) - m_old
        p = jnp.exp2(st - delta).astype(BF16)
        pv = _dot(v1, p)
        acc_scr[g] = pv if first else acc_scr[g] * jnp.exp2(-delta) + pv
        m_scr[g] = m_old + delta

    @pl.when(ki == 0)
    def _():
        vc1 = _with_ones(vc_ref[...])
        kc = kc_ref[...]
        for g in range(GROUP):
            two_pass_update(g, kc, vc1, True)

    v1 = _with_ones(v_ref[...])
    k = k_ref[...]
    for g in range(GROUP):
        st = _dot(k, q_ext(g, m_scr[g]))
        p = jnp.exp2(st.astype(BF16))
        pmax_scr[g] = jnp.max(p, axis=0, keepdims=True).astype(F32)
        new_scr[g] = acc_scr[g] + _dot(v1, p)
    clean = jnp.max(pmax_scr[...]) <= 2.0 ** DIRTY_LOG2

    @pl.when(clean)
    def _():
        for g in range(GROUP):
            m_old = m_scr[g]
            m_new = _round_bf16(m_old + jnp.log2(jnp.maximum(pmax_scr[g], 1.0)))
            acc_scr[g] = new_scr[g] * jnp.exp2(m_old - m_new)
            m_scr[g] = m_new

    @pl.when(jnp.logical_not(clean))
    def _():
        for g in range(GROUP):
            two_pass_update(g, k, v1, False)

    @pl.when(ki == pl.num_programs(3) - 1)
    def _():
        for g in range(GROUP):
            a = acc_scr[g]
            o_ref[g * HEAD_DIM:(g + 1) * HEAD_DIM, :] = (a[0:HEAD_DIM] / a[HEAD_DIM:HEAD_DIM + 1]).astype(BF16)


def _global_attention(qt, k, vt, dims, tq, tk):
    nb, s, n_ctx = dims["batch"], dims["seq"], dims["n_ctx"]
    nq, nk = s // tq, s // tk
    ctx_blk0 = (nb * s) // n_ctx
    return pl.pallas_call(
        _flash_kernel,
        grid=(nb, KV_HEADS, nq, nk),
        in_specs=[pl.BlockSpec((GROUP, LANES, tq), lambda b, kv, qi, ki: (kv, 0, b * nq + qi)),
                  pl.BlockSpec((None, tk, LANES), lambda b, kv, qi, ki: (kv, b * nk + ki, 0)),
                  pl.BlockSpec((None, HEAD_DIM, tk), lambda b, kv, qi, ki: (kv, 0, b * nk + ki)),
                  pl.BlockSpec((None, n_ctx, LANES), lambda b, kv, qi, ki: (kv, ctx_blk0 + b, 0)),
                  pl.BlockSpec((None, HEAD_DIM, n_ctx), lambda b, kv, qi, ki: (kv, 0, ctx_blk0 + b))],
        out_specs=pl.BlockSpec((GROUP * HEAD_DIM, tq), lambda b, kv, qi, ki: (kv, b * nq + qi)),
        out_shape=jax.ShapeDtypeStruct((N_HEADS_MIXER * HEAD_DIM, nb * s), BF16),
        scratch_shapes=[pltpu.VMEM((GROUP, 1, tq), F32), pltpu.VMEM((GROUP, V_ROWS, tq), F32),
                        pltpu.VMEM((GROUP, V_ROWS, tq), F32), pltpu.VMEM((GROUP, 1, tq), F32)],
        compiler_params=pltpu.CompilerParams(
            dimension_semantics=("arbitrary", "arbitrary", "arbitrary", "arbitrary"), vmem_limit_bytes=VMEM_LIMIT),
        name="global_attention",
    )(qt, k, vt, k, vt)


def _single_pass_attend(k_all, v1_all, q_t, mask, sink, exp=jnp.exp):
    st = _dot(k_all, q_t)
    if mask is not None:
        st = jnp.where(mask, st, NEG_INF)
    m = jnp.max(st, axis=0, keepdims=True)
    if sink is not None:
        m = jnp.maximum(m, sink)
    p = exp(st - m).astype(BF16)
    a = _dot(v1_all, p)
    den = a[HEAD_DIM:HEAD_DIM + 1]
    if sink is not None:
        den = den + exp(sink - m)
    return a[0:HEAD_DIM] / den


def _local_kernel(sink_ref, q_ref, kp_ref, km_ref, kn_ref, kc_ref, vp_ref, vm_ref, vn_ref, vc_ref, o_ref):
    kv = pl.program_id(1)
    qi = pl.program_id(2)
    nq = pl.num_programs(2)
    tq = q_ref.shape[2]
    n_ctx = kc_ref.shape[0]
    span = tq + 2 * WINDOW
    k_all = jnp.concatenate([kp_ref[...], km_ref[...], kn_ref[...], kc_ref[...]], axis=0)
    v1_all = _with_ones(jnp.concatenate([vp_ref[...], vm_ref[...], vn_ref[...], vc_ref[...]], axis=1))
    shape = (span + n_ctx, tq)
    j = lax.broadcasted_iota(jnp.int32, shape, 0)
    i = lax.broadcasted_iota(jnp.int32, shape, 1)
    band = jnp.abs(j - WINDOW - i) <= WINDOW
    qi_v = jnp.full(shape, qi, jnp.int32)
    in_seq = ((j >= WINDOW) | (qi_v > 0)) & ((j < WINDOW + tq) | (qi_v < nq - 1))
    mask = (band & in_seq & (j < span)) | (j >= span)
    for g in range(GROUP):
        sink = jnp.full((1, tq), sink_ref[kv * GROUP + g], F32)
        o = _single_pass_attend(k_all, v1_all, q_ref[g], mask, sink)
        o_ref[g * HEAD_DIM:(g + 1) * HEAD_DIM, :] = o.astype(BF16)


def _local_attention(sink, qt, k, vt, dims, tq):
    nb, s, n_ctx = dims["batch"], dims["seq"], dims["n_ctx"]
    nq = s // tq
    w = WINDOW
    per_b_w = s // w
    r = tq // w
    ctx_blk0 = (nb * s) // n_ctx

    def prev_idx(b, qi):
        return b * per_b_w + jnp.maximum(qi * r - 1, 0)

    def next_idx(b, qi):
        return b * per_b_w + jnp.minimum(qi * r + r, per_b_w - 1)

    return pl.pallas_call(
        _local_kernel,
        grid=(nb, KV_HEADS, nq),
        in_specs=[pl.BlockSpec(memory_space=pltpu.SMEM),
                  pl.BlockSpec((GROUP, LANES, tq), lambda b, kv, qi: (kv, 0, b * nq + qi)),
                  pl.BlockSpec((None, w, LANES), lambda b, kv, qi: (kv, prev_idx(b, qi), 0)),
                  pl.BlockSpec((None, tq, LANES), lambda b, kv, qi: (kv, b * nq + qi, 0)),
                  pl.BlockSpec((None, w, LANES), lambda b, kv, qi: (kv, next_idx(b, qi), 0)),
                  pl.BlockSpec((None, n_ctx, LANES), lambda b, kv, qi: (kv, ctx_blk0 + b, 0)),
                  pl.BlockSpec((None, HEAD_DIM, w), lambda b, kv, qi: (kv, 0, prev_idx(b, qi))),
                  pl.BlockSpec((None, HEAD_DIM, tq), lambda b, kv, qi: (kv, 0, b * nq + qi)),
                  pl.BlockSpec((None, HEAD_DIM, w), lambda b, kv, qi: (kv, 0, next_idx(b, qi))),
                  pl.BlockSpec((None, HEAD_DIM, n_ctx), lambda b, kv, qi: (kv, 0, ctx_blk0 + b))],
        out_specs=pl.BlockSpec((GROUP * HEAD_DIM, tq), lambda b, kv, qi: (kv, b * nq + qi)),
        out_shape=jax.ShapeDtypeStruct((N_HEADS_MIXER * HEAD_DIM, nb * s), BF16),
        compiler_params=pltpu.CompilerParams(
            dimension_semantics=("arbitrary", "arbitrary", "arbitrary"), vmem_limit_bytes=VMEM_LIMIT),
        name="local_attention",
    )(sink, qt, k, k, k, k, vt, vt, vt, vt)


def _ctx_kernel(sink_ref, qg_ref, kg_ref, vg_ref, ql_ref, kl_ref, vl_ref, o_ref):
    tq = qg_ref.shape[2]
    for mixer, (q_ref, k_ref, v_ref) in enumerate(((qg_ref, kg_ref, vg_ref), (ql_ref, kl_ref, vl_ref))):
        for kv in range(KV_HEADS):
            k_all = k_ref[kv]
            v1 = _with_ones(v_ref[kv])
            for g in range(GROUP):
                hq = kv * GROUP + g
                sink = jnp.full((1, tq), sink_ref[hq], F32) if mixer == 1 else None
                o = _single_pass_attend(k_all, v1, q_ref[hq], None, sink, jnp.exp2 if mixer == 0 else jnp.exp)
                row = (mixer * N_HEADS_MIXER + hq) * HEAD_DIM
                o_ref[row:row + HEAD_DIM, :] = o.astype(BF16)


def _ctx_attention(sink, qgt, kg, vgt, qlt, kl, vlt, dims):
    nb, s, n_ctx = dims["batch"], dims["seq"], dims["n_ctx"]
    ctx_blk0 = (nb * s) // n_ctx
    q_spec = pl.BlockSpec((N_HEADS_MIXER, LANES, n_ctx), lambda b: (0, 0, ctx_blk0 + b))
    k_spec = pl.BlockSpec((KV_HEADS, n_ctx, LANES), lambda b: (0, ctx_blk0 + b, 0))
    v_spec = pl.BlockSpec((KV_HEADS, HEAD_DIM, n_ctx), lambda b: (0, 0, ctx_blk0 + b))
    rows = 2 * N_HEADS_MIXER * HEAD_DIM
    return pl.pallas_call(
        _ctx_kernel,
        grid=(nb,),
        in_specs=[pl.BlockSpec(memory_space=pltpu.SMEM), q_spec, k_spec, v_spec, q_spec, k_spec, v_spec],
        out_specs=pl.BlockSpec((rows, n_ctx), lambda b: (0, b)),
        out_shape=jax.ShapeDtypeStruct((rows, nb * n_ctx), BF16),
        compiler_params=pltpu.CompilerParams(dimension_semantics=("arbitrary",), vmem_limit_bytes=VMEM_LIMIT),
        name="ctx_attention",
    )(sink, qgt, kg, vgt, qlt, kl, vlt)


def _route(sel, scores):
    n = EXPERTS_PER_GROUP
    group_score = []
    for g in range(N_GROUPS):
        r = sel[g * n:(g + 1) * n]
        best = None
        for a in range(n):
            for b in range(a + 1, n):
                pair = r[a] + r[b]
                best = pair if best is None else jnp.maximum(best, pair)
        group_score.append(best)
    gates = [None] * N_EXPERTS
    picked = []
    for g in range(N_GROUPS):
        chosen = None
        for h in range(N_GROUPS):
            if h == g:
                continue
            c = (group_score[g] > group_score[h]) if h < g else (group_score[g] >= group_score[h])
            chosen = c if chosen is None else (chosen & c)
        for a in range(n):
            ea = g * n + a
            rank = jnp.zeros_like(sel[ea])
            for b in range(n):
                if b == a:
                    continue
                eb = g * n + b
                ahead = (sel[eb] >= sel[ea]) if b < a else (sel[eb] > sel[ea])
                rank = rank + jnp.where(ahead, 1.0, 0.0)
            picked.append(chosen & (rank < 2.0))
    den = None
    for e in range(N_EXPERTS):
        w = jnp.where(picked[e], scores[e], 0.0)
        gates[e] = w
        den = w if den is None else den + w
    return [w / den for w in gates]


def _post_kernel(yg_ref, yl_ref, yc_ref, x_ref, mod_ref, wout_ref, g_ref, rwt_ref, rb_ref,
                 x1_ref, h2_ref, gates_ref, *, n_lat):
    i = pl.program_id(0)
    d = x_ref.shape[1]
    tm = x_ref.shape[0]
    yt_lat = jnp.concatenate([yg_ref[...], yl_ref[...]], axis=0)
    is_lat = jnp.full(yt_lat.shape, i, jnp.int32) < n_lat
    yt = jnp.where(is_lat, yt_lat, yc_ref[...])
    a = lax.dot_general(yt, wout_ref[...], (((0,), (0,)), ((), ())), preferred_element_type=F32)
    gate1 = mod_ref[:, 2 * d:3 * d]
    shift2 = mod_ref[:, 3 * d:4 * d]
    scale2 = mod_ref[:, 4 * d:5 * d]
    x1 = x_ref[...] + gate1 * a
    x1_ref[...] = x1
    h2 = (_rms_rows(x1, g_ref[...]) * (1.0 + scale2) + shift2).astype(BF16)
    h2_ref[...] = h2
    logits_t = lax.dot_general(rwt_ref[...], h2, (((1,), (1,)), ((), ())), preferred_element_type=F32)
    scores_t = _sigmoid(logits_t)
    sel_t = scores_t + rb_ref[...]
    gates = _route([sel_t[e:e + 1] for e in range(N_EXPERTS)], [scores_t[e:e + 1] for e in range(N_EXPERTS)])
    gates_t = jnp.concatenate(gates + [jnp.zeros((LANES - N_EXPERTS, tm), F32)], axis=0)
    gates_ref[...] = gates_t.T


def _post_attention(ygt, ylt, yct, x_all, mod, l, w_out_b, ffn_g, rw_t, rb_col, dims):
    ntok, d = x_all.shape
    tm = TOKEN_TILE
    n_lat, per_b, nb = dims["n_lat_tiles"], dims["tiles_per_batch"], dims["batch"]
    rows = ygt.shape[0]

    def mod_idx(i):
        return (l, jnp.minimum(i // per_b, nb), 0, 0)

    def lat_idx(i):
        return (0, jnp.minimum(i, n_lat - 1))

    return pl.pallas_call(
        functools.partial(_post_kernel, n_lat=n_lat),
        grid=(ntok // tm,),
        in_specs=[pl.BlockSpec((rows, tm), lat_idx),
                  pl.BlockSpec((rows, tm), lat_idx),
                  pl.BlockSpec((2 * rows, tm), lambda i: (0, 0)),
                  pl.BlockSpec((tm, d), lambda i: (i, 0)),
                  pl.BlockSpec((None, None, 1, mod.shape[-1]), mod_idx),
                  pl.BlockSpec(w_out_b.shape, lambda i: (0, 0)),
                  pl.BlockSpec((1, d), lambda i: (0, 0)),
                  pl.BlockSpec(rw_t.shape, lambda i: (0, 0)),
                  pl.BlockSpec(rb_col.shape, lambda i: (0, 0))],
        out_specs=[pl.BlockSpec((tm, d), lambda i: (i, 0)),
                   pl.BlockSpec((tm, d), lambda i: (i, 0)),
                   pl.BlockSpec((tm, LANES), lambda i: (i, 0))],
        out_shape=[jax.ShapeDtypeStruct((ntok, d), F32),
                   jax.ShapeDtypeStruct((ntok, d), BF16),
                   jax.ShapeDtypeStruct((ntok, LANES), F32)],
        compiler_params=pltpu.CompilerParams(dimension_semantics=("arbitrary",), vmem_limit_bytes=VMEM_LIMIT),
        name="post_attention",
    )(ygt, ylt, yct, x_all, mod, w_out_b, ffn_g, rw_t, rb_col)


def _moe_kernel(h_ref, gates_ref, wg_ref, wu_ref, wd_ref, x1_ref, mod_ref, o_ref, acc_scr):
    e = pl.program_id(1)
    d = x1_ref.shape[1]

    @pl.when(e == 0)
    def _():
        acc_scr[...] = jnp.zeros_like(acc_scr)

    h = h_ref[...]
    a = _dot(h, wg_ref[0, 0])
    u = _dot(h, wu_ref[0, 0])
    gates = gates_ref[...]
    lane = lax.broadcasted_iota(jnp.int32, gates.shape, 1)
    gcol = jnp.sum(jnp.where(lane == e, gates, 0.0), axis=1, keepdims=True)
    he = (a * _sigmoid(a)) * u * gcol
    acc_scr[...] += _dot(he.astype(BF16), wd_ref[0, 0])

    @pl.when(e == pl.num_programs(1) - 1)
    def _():
        gate2 = mod_ref[:, 5 * d:6 * d]
        o_ref[...] = x1_ref[...] + gate2 * acc_scr[...]


def _moe(h2, gates, wg_b, wu_b, wd_b, x1, mod, l, dims):
    ntok, d = x1.shape
    tm = TOKEN_TILE
    f = wg_b.shape[-1]
    per_b, nb = dims["tiles_per_batch"], dims["batch"]

    def mod_idx(i, e):
        return (l, jnp.minimum(i // per_b, nb), 0, 0)

    return pl.pallas_call(
        _moe_kernel,
        grid=(ntok // tm, N_EXPERTS),
        in_specs=[pl.BlockSpec((tm, d), lambda i, e: (i, 0)),
                  pl.BlockSpec((tm, LANES), lambda i, e: (i, 0)),
                  pl.BlockSpec((1, 1, d, f), lambda i, e: (l, e, 0, 0)),
                  pl.BlockSpec((1, 1, d, f), lambda i, e: (l, e, 0, 0)),
                  pl.BlockSpec((1, 1, f, d), lambda i, e: (l, e, 0, 0)),
                  pl.BlockSpec((tm, d), lambda i, e: (i, 0)),
                  pl.BlockSpec((None, None, 1, mod.shape[-1]), mod_idx)],
        out_specs=pl.BlockSpec((tm, d), lambda i, e: (i, 0)),
        out_shape=jax.ShapeDtypeStruct((ntok, d), F32),
        scratch_shapes=[pltpu.VMEM((tm, d), F32)],
        compiler_params=pltpu.CompilerParams(dimension_semantics=("arbitrary", "arbitrary"),
                                             vmem_limit_bytes=VMEM_LIMIT),
        name="moe_dense",
    )(h2, gates, wg_b, wu_b, wd_b, x1, mod)


def _rope_tables(s, pad):
    rows = s // GRID_W
    row = jnp.repeat(jnp.arange(rows, dtype=jnp.int32), GRID_W).astype(F32)
    col = jnp.tile(jnp.arange(GRID_W, dtype=jnp.int32), rows).astype(F32)
    inv = ROPE_THETA ** (-jnp.arange(N_FREQ, dtype=F32) / N_FREQ)
    ang_r, ang_c = row[:, None] * inv, col[:, None] * inv
    cr, sr, cc, sc = jnp.cos(ang_r), jnp.sin(ang_r), jnp.cos(ang_c), jnp.sin(ang_c)
    one, zero = jnp.ones((pad, N_FREQ), F32), jnp.zeros((pad, N_FREQ), F32)
    cs_t = jnp.concatenate([jnp.concatenate([cr, sr, cc, sc], axis=1),
                            jnp.concatenate([one, zero, one, zero], axis=1)], axis=0).T
    cpat = jnp.concatenate([cr, cr, cc, cc], axis=1)
    spat = jnp.concatenate([-sr, sr, -sc, sc], axis=1)
    ident = jnp.concatenate([jnp.ones((pad, 2 * HEAD_DIM), F32), jnp.zeros((pad, 2 * HEAD_DIM), F32)], axis=1)
    cs_nat = jnp.concatenate([jnp.concatenate([cpat, cpat, spat, spat], axis=1), ident], axis=0)
    return cs_nat, cs_t


def kernel(x, c, ctx, c_ctx, ada_w, ada_b, attn_norm_g, ffn_norm_g, w_in, qn_global, kn_global, qn_local, kn_local,
           sink_logit, w_out, router_w, router_b, w_gate, w_up, w_down):
    nb, s, d = x.shape
    n_ctx = ctx.shape[1]
    depth = ada_w.shape[0]
    tm = TOKEN_TILE
    assert s % tm == 0 and nb * n_ctx == tm and s % GRID_W == 0 and nb + 1 <= MOD_ROWS
    dims = dict(batch=nb, seq=s, n_ctx=n_ctx, tiles_per_batch=s // tm, n_lat_tiles=nb * s // tm)

    cvec = jnp.concatenate([c, c_ctx[None, :], jnp.zeros((MOD_ROWS - nb - 1, d), F32)], axis=0)
    mod = _modulation(cvec, ada_w, ada_b)[:, :nb + 1].reshape(depth, nb + 1, 1, 6 * d)

    cs_nat, cs_t = _rope_tables(s, tm)
    head_of_lane = np.arange(LANES) // HEAD_DIM
    block_ones = jnp.asarray(head_of_lane[:, None] == head_of_lane[None, :], BF16)
    w_in_b = w_in.astype(BF16)
    w_out_b = w_out.astype(BF16)
    wg_b, wu_b, wd_b = w_gate.astype(BF16), w_up.astype(BF16), w_down.astype(BF16)
    rw_t = router_w.T.astype(BF16)
    rb_col = router_b.reshape(N_EXPERTS, 1).astype(F32)

    x_all = jnp.concatenate([x.reshape(nb * s, d), ctx.reshape(nb * n_ctx, d)], axis=0)
    tq_g, tk_g = min(512, s), min(2048, s)
    tq_l = 256
    for l in range(depth):
        gq_cols = jnp.stack([qn_global[l], qn_local[l]]).reshape(2, HEAD_DIM, 1)
        gk_rows = jnp.stack([jnp.tile(kn_global[l], KV_HEADS), jnp.tile(kn_local[l], KV_HEADS)]).reshape(2, 1, LANES)
        qgt, kg, vgt, qlt, kl, vlt = _pre_attention(
            x_all, mod, l, attn_norm_g[l].reshape(1, d), w_in_b[l], gq_cols, gk_rows, block_ones, cs_nat, cs_t, dims)
        ygt = _global_attention(qgt, kg, vgt, dims, tq_g, tk_g)
        ylt = _local_attention(sink_logit[l], qlt, kl, vlt, dims, tq_l)
        yct = _ctx_attention(sink_logit[l], qgt, kg, vgt, qlt, kl, vlt, dims)
        x1, h2, gates = _post_attention(ygt, ylt, yct, x_all, mod, l, w_out_b[l], ffn_norm_g[l].reshape(1, d),
                                        rw_t, rb_col, dims)
        x_all = _moe(h2, gates, wg_b, wu_b, wd_b, x1, mod, l, dims)
    return x_all[:nb * s].reshape(nb, s, d)
```

```python
import functools
import math

import numpy as np
import jax
import jax.numpy as jnp
from jax import lax
from jax.experimental import pallas as pl
from jax.experimental.pallas import tpu as pltpu

F32 = jnp.float32
BF16 = jnp.bfloat16

HEAD_DIM = 64
GRID_W = 64
WINDOW = 128
N_HEADS_MIXER = 8
KV_HEADS = 2
GROUP = N_HEADS_MIXER // KV_HEADS
ROPE_THETA = 10000.0
N_EXPERTS = 16
N_GROUPS = 4
EXPERTS_PER_GROUP = N_EXPERTS // N_GROUPS
EXPERT_PAIRS = tuple((a, b) for a in range(EXPERTS_PER_GROUP) for b in range(a + 1, EXPERTS_PER_GROUP))
N_CLASSES = N_GROUPS * len(EXPERT_PAIRS)
NORM_EPS = 1e-6
ATTN_SCALE = 1.0 / math.sqrt(HEAD_DIM)
LOG2_E = 1.4426950408889634
Q_SCALES = (ATTN_SCALE * LOG2_E, ATTN_SCALE)
NEG_INF = -1e30
N_FREQ = HEAD_DIM // 4

LANES = 128
TOKEN_TILE = 512
MOE_TILE = 256
DIRTY_LOG2 = 60.0
ONES_ROWS = 16
V_ROWS = HEAD_DIM + ONES_ROWS
VMEM_LIMIT = 48 * 1024 * 1024
MOD_ROWS = 16


def _dot(a, b):
    return jnp.dot(a, b, preferred_element_type=F32)


def _sigmoid(x):
    return 1.0 / (1.0 + jnp.exp(-x))


def _mod_kernel(c_ref, w_ref, b_ref, o_ref):
    c = c_ref[...]
    s = (c * _sigmoid(c)).astype(BF16)
    o_ref[0] = _dot(s, w_ref[0].astype(BF16)) + b_ref[0]


def _modulation(cvec, ada_w, ada_b):
    depth, d, d6 = ada_w.shape
    col = 1536
    assert d6 % col == 0
    return pl.pallas_call(
        _mod_kernel,
        grid=(depth, d6 // col),
        in_specs=[pl.BlockSpec((MOD_ROWS, d), lambda l, j: (0, 0)),
                  pl.BlockSpec((1, d, col), lambda l, j: (l, 0, j)),
                  pl.BlockSpec((1, 1, col), lambda l, j: (l, 0, j))],
        out_specs=pl.BlockSpec((1, MOD_ROWS, col), lambda l, j: (l, 0, j)),
        out_shape=jax.ShapeDtypeStruct((depth, MOD_ROWS, d6), F32),
        compiler_params=pltpu.CompilerParams(dimension_semantics=("arbitrary", "arbitrary"),
                                             vmem_limit_bytes=VMEM_LIMIT),
        name="modulation",
    )(cvec, ada_w, ada_b.reshape(depth, 1, d6))


def _rms_rows(x, gain):
    ms = jnp.mean(x * x, axis=-1, keepdims=True)
    return x * lax.rsqrt(ms + NORM_EPS) * gain


def _rope_t(t, cs_t):
    n = N_FREQ
    cr, sr, cc, sc = cs_t[0:n], cs_t[n:2 * n], cs_t[2 * n:3 * n], cs_t[3 * n:4 * n]
    x1r, x2r, x1c, x2c = t[0:n], t[n:2 * n], t[2 * n:3 * n], t[3 * n:4 * n]
    return jnp.concatenate([x1r * cr - x2r * sr, x2r * cr + x1r * sr,
                            x1c * cc - x2c * sc, x2c * cc + x1c * sc], axis=0)


def _q_heads_t(y_pair, gain_col, cs_t, q_scale):
    t = y_pair.T
    outs = []
    for j in range(2):
        th = t[j * HEAD_DIM:(j + 1) * HEAD_DIM]
        ms = jnp.mean(th * th, axis=0, keepdims=True)
        thn = th * lax.rsqrt(ms + NORM_EPS) * gain_col
        outs.append(_rope_t(thn, cs_t) * q_scale)
    return outs


def _k_heads(kc, block_ones, gain_row, cs):
    sq = kc * kc
    hi = sq.astype(BF16)
    lo = (sq - hi.astype(F32)).astype(BF16)
    ss = _dot(hi, block_ones) + _dot(lo, block_ones)
    kn = kc * lax.rsqrt(ss * (1.0 / HEAD_DIM) + NORM_EPS) * gain_row
    lane = lax.broadcasted_iota(jnp.int32, kn.shape, 1)
    first_half = (lane % (2 * N_FREQ)) < N_FREQ
    swapped = jnp.where(first_half, pltpu.roll(kn, LANES - N_FREQ, 1), pltpu.roll(kn, N_FREQ, 1))
    return kn * cs[:, :LANES] + swapped * cs[:, LANES:]


def _pre_kernel(*refs, combine):
    if combine:
        x1_ref, yf_ref, modp_ref = refs[:3]
        refs = refs[3:]
        xo_ref = refs[-1]
        refs = refs[:-1]
        d = x1_ref.shape[1]
        x = x1_ref[...] + modp_ref[:, 5 * d:6 * d] * yf_ref[...]
        xo_ref[...] = x
    else:
        x = refs[0][...]
        refs = refs[1:]
    (mod_ref, g_ref, w_ref, gq_ref, gk_ref, bones_ref, cs_ref, cst_ref,
     qgt_ref, kg_ref, vgt_ref, qlt_ref, kl_ref, vlt_ref) = refs
    tm, d = x.shape
    shift = mod_ref[:, 0:d]
    scale = mod_ref[:, d:2 * d]
    h = _rms_rows(x, g_ref[...]) * (1.0 + scale) + shift
    y = _dot(h.astype(BF16), w_ref[...])
    cs_t = cst_ref[...]
    cs = cs_ref[...]
    zeros = jnp.zeros((HEAD_DIM, tm), BF16)
    qw = N_HEADS_MIXER * HEAD_DIM
    kw = KV_HEADS * HEAD_DIM
    for mixer, (qt_ref, k_ref, vt_ref) in enumerate(((qgt_ref, kg_ref, vgt_ref), (qlt_ref, kl_ref, vlt_ref))):
        base = mixer * (qw + 2 * kw)
        gain_col = gq_ref[mixer]
        q_scale = Q_SCALES[mixer]
        for c in range(N_HEADS_MIXER // 2):
            pair = _q_heads_t(y[:, base + c * LANES: base + (c + 1) * LANES], gain_col, cs_t, q_scale)
            for j in range(2):
                qt_ref[2 * c + j] = jnp.concatenate([pair[j].astype(BF16), zeros], axis=0)
        kc = y[:, base + qw: base + qw + kw]
        kboth = _k_heads(kc, bones_ref[...], gk_ref[mixer], cs)
        lane = lax.broadcasted_iota(jnp.int32, kboth.shape, 1)
        ones_col = jnp.where(lane == HEAD_DIM, 1.0, 0.0)
        k_ref[0] = jnp.where(lane < HEAD_DIM, kboth, ones_col).astype(BF16)
        k_ref[1] = jnp.where(lane < HEAD_DIM, pltpu.roll(kboth, HEAD_DIM, 1), ones_col).astype(BF16)
        vc = y[:, base + qw + kw: base + qw + 2 * kw]
        vt = vc.T.astype(BF16)
        for kv in range(KV_HEADS):
            vt_ref[kv] = vt[kv * HEAD_DIM:(kv + 1) * HEAD_DIM]


def _pre_attention(stream, mod, l, attn_g, w_in_b, gq_cols, gk_rows, block_ones, cs_nat, cs_t, dims):
    combine = len(stream) == 2
    ntok, d = stream[0].shape
    tm = TOKEN_TILE
    n_lat, per_b, nb = dims["n_lat_tiles"], dims["tiles_per_batch"], dims["batch"]
    n_tiles = ntok // tm

    def mod_idx(i):
        return (l, jnp.minimum(i // per_b, nb), 0, 0)

    def tab_idx(i):
        return jnp.where(i < n_lat, i % per_b, per_b)

    qt_shape = jax.ShapeDtypeStruct((N_HEADS_MIXER, LANES, ntok), BF16)
    k_shape = jax.ShapeDtypeStruct((KV_HEADS, ntok, LANES), BF16)
    vt_shape = jax.ShapeDtypeStruct((KV_HEADS, HEAD_DIM, ntok), BF16)
    qt_spec = pl.BlockSpec((N_HEADS_MIXER, LANES, tm), lambda i: (0, 0, i))
    k_spec = pl.BlockSpec((KV_HEADS, tm, LANES), lambda i: (0, i, 0))
    vt_spec = pl.BlockSpec((KV_HEADS, HEAD_DIM, tm), lambda i: (0, 0, i))
    row_spec = pl.BlockSpec((tm, d), lambda i: (i, 0))
    mod_spec = pl.BlockSpec((None, None, 1, mod.shape[-1]), mod_idx)
    if combine:
        modp_spec = pl.BlockSpec((None, None, 1, mod.shape[-1]), lambda i: (l - 1,) + mod_idx(i)[1:])
        stream_specs, stream_args = [row_spec, row_spec, modp_spec], [stream[0], stream[1], mod]
        extra_specs, extra_shapes = [row_spec], [jax.ShapeDtypeStruct((ntok, d), F32)]
    else:
        stream_specs, stream_args, extra_specs, extra_shapes = [row_spec], [stream[0]], [], []
    return pl.pallas_call(
        functools.partial(_pre_kernel, combine=combine),
        grid=(n_tiles,),
        in_specs=stream_specs + [
                  mod_spec,
                  pl.BlockSpec((1, d), lambda i: (0, 0)),
                  pl.BlockSpec(w_in_b.shape, lambda i: (0, 0)),
                  pl.BlockSpec(gq_cols.shape, lambda i: (0, 0, 0)),
                  pl.BlockSpec(gk_rows.shape, lambda i: (0, 0, 0)),
                  pl.BlockSpec(block_ones.shape, lambda i: (0, 0)),
                  pl.BlockSpec((tm, 2 * LANES), lambda i: (tab_idx(i), 0)),
                  pl.BlockSpec((HEAD_DIM, tm), lambda i: (0, tab_idx(i)))],
        out_specs=[qt_spec, k_spec, vt_spec, qt_spec, k_spec, vt_spec] + extra_specs,
        out_shape=[qt_shape, k_shape, vt_shape, qt_shape, k_shape, vt_shape] + extra_shapes,
        compiler_params=pltpu.CompilerParams(dimension_semantics=("arbitrary",), vmem_limit_bytes=VMEM_LIMIT),
        name="pre_attention",
    )(*stream_args, mod, attn_g, w_in_b, gq_cols, gk_rows, block_ones, cs_nat, cs_t)


def _with_ones(vt):
    return jnp.concatenate([vt, jnp.ones((ONES_ROWS, vt.shape[1]), BF16)], axis=0)


def _round_bf16(x):
    return x.astype(BF16).astype(F32)


def _flash_kernel(q_ref, k_ref, v_ref, kc_ref, vc_ref, o_ref, m_scr, acc_scr, new_scr, pmax_scr):
    ki = pl.program_id(3)
    tq = q_ref.shape[2]
    zpad = jnp.zeros((LANES - HEAD_DIM - ONES_ROWS, tq), BF16)

    def q_ext(g, m_row):
        neg = jnp.concatenate([-m_row, jnp.zeros((ONES_ROWS - 1, tq), F32)], axis=0).astype(BF16)
        return jnp.concatenate([q_ref[g, 0:HEAD_DIM], neg, zpad], axis=0)

    def two_pass_update(g, k, v1, first):
        m_old = jnp.zeros((1, tq), F32) if first else m_scr[g]
        st = _dot(k, q_ext(g, m_old))
        blk = jnp.max(st, axis=0, keepdims=True)
        delta = _round_bf16(m_old + (blk if first else jnp.maximum(blk, 0.0))) - m_old
        p = jnp.exp2(st - delta).astype(BF16)
        pv = _dot(v1, p)
        acc_scr[g] = pv if first else acc_scr[g] * jnp.exp2(-delta) + pv
        m_scr[g] = m_old + delta

    @pl.when(ki == 0)
    def _():
        vc1 = _with_ones(vc_ref[...])
        kc = kc_ref[...]
        for g in range(GROUP):
            two_pass_update(g, kc, vc1, True)

    v1 = _with_ones(v_ref[...])
    k = k_ref[...]
    for g in range(GROUP):
        st = _dot(k, q_ext(g, m_scr[g]))
        p = jnp.exp2(st.astype(BF16))
        pmax_scr[g] = jnp.max(p, axis=0, keepdims=True).astype(F32)
        new_scr[g] = acc_scr[g] + _dot(v1, p)
    clean = jnp.max(pmax_scr[...]) <= 2.0 ** DIRTY_LOG2

    @pl.when(clean)
    def _():
        for g in range(GROUP):
            m_old = m_scr[g]
            m_new = _round_bf16(m_old + jnp.log2(jnp.maximum(pmax_scr[g], 1.0)))
            acc_scr[g] = new_scr[g] * jnp.exp2(m_old - m_new)
            m_scr[g] = m_new

    @pl.when(jnp.logical_not(clean))
    def _():
        for g in range(GROUP):
            two_pass_update(g, k, v1, False)

    @pl.when(ki == pl.num_programs(3) - 1)
    def _():
        for g in range(GROUP):
            a = acc_scr[g]
            o_ref[g * HEAD_DIM:(g + 1) * HEAD_DIM, :] = (a[0:HEAD_DIM] / a[HEAD_DIM:HEAD_DIM + 1]).astype(BF16)


def _global_attention(qt, k, vt, dims, tq, tk):
    nb, s, n_ctx = dims["batch"], dims["seq"], dims["n_ctx"]
    nq, nk = s // tq, s // tk
    ctx_blk0 = (nb * s) // n_ctx
    return pl.pallas_call(
        _flash_kernel,
        grid=(nb, KV_HEADS, nq, nk),
        in_specs=[pl.BlockSpec((GROUP, LANES, tq), lambda b, kv, qi, ki: (kv, 0, b * nq + qi)),
                  pl.BlockSpec((None, tk, LANES), lambda b, kv, qi, ki: (kv, b * nk + ki, 0)),
                  pl.BlockSpec((None, HEAD_DIM, tk), lambda b, kv, qi, ki: (kv, 0, b * nk + ki)),
                  pl.BlockSpec((None, n_ctx, LANES), lambda b, kv, qi, ki: (kv, ctx_blk0 + b, 0)),
                  pl.BlockSpec((None, HEAD_DIM, n_ctx), lambda b, kv, qi, ki: (kv, 0, ctx_blk0 + b))],
        out_specs=pl.BlockSpec((GROUP * HEAD_DIM, tq), lambda b, kv, qi, ki: (kv, b * nq + qi)),
        out_shape=jax.ShapeDtypeStruct((N_HEADS_MIXER * HEAD_DIM, nb * s), BF16),
        scratch_shapes=[pltpu.VMEM((GROUP, 1, tq), F32), pltpu.VMEM((GROUP, V_ROWS, tq), F32),
                        pltpu.VMEM((GROUP, V_ROWS, tq), F32), pltpu.VMEM((GROUP, 1, tq), F32)],
        compiler_params=pltpu.CompilerParams(
            dimension_semantics=("arbitrary", "arbitrary", "arbitrary", "arbitrary"), vmem_limit_bytes=VMEM_LIMIT),
        name="global_attention",
    )(qt, k, vt, k, vt)


def _single_pass_attend(k_all, v1_all, q_t, mask, sink, exp=jnp.exp):
    st = _dot(k_all, q_t)
    if mask is not None:
        st = jnp.where(mask, st, NEG_INF)
    m = jnp.max(st, axis=0, keepdims=True)
    if sink is not None:
        m = jnp.maximum(m, sink)
    p = exp(st - m).astype(BF16)
    a = _dot(v1_all, p)
    den = a[HEAD_DIM:HEAD_DIM + 1]
    if sink is not None:
        den = den + exp(sink - m)
    return a[0:HEAD_DIM] / den


def _local_kernel(sink_ref, q_ref, kp_ref, km_ref, kn_ref, kc_ref, vp_ref, vm_ref, vn_ref, vc_ref, o_ref):
    kv = pl.program_id(1)
    qi = pl.program_id(2)
    nq = pl.num_programs(2)
    tq = q_ref.shape[2]
    n_ctx = kc_ref.shape[0]
    span = tq + 2 * WINDOW
    k_all = jnp.concatenate([kp_ref[...], km_ref[...], kn_ref[...], kc_ref[...]], axis=0)
    v1_all = _with_ones(jnp.concatenate([vp_ref[...], vm_ref[...], vn_ref[...], vc_ref[...]], axis=1))
    shape = (span + n_ctx, tq)
    j = lax.broadcasted_iota(jnp.int32, shape, 0)
    i = lax.broadcasted_iota(jnp.int32, shape, 1)
    band = jnp.abs(j - WINDOW - i) <= WINDOW
    qi_v = jnp.full(shape, qi, jnp.int32)
    in_seq = ((j >= WINDOW) | (qi_v > 0)) & ((j < WINDOW + tq) | (qi_v < nq - 1))
    mask = (band & in_seq & (j < span)) | (j >= span)
    for g in range(GROUP):
        sink = jnp.full((1, tq), sink_ref[kv * GROUP + g], F32)
        o = _single_pass_attend(k_all, v1_all, q_ref[g], mask, sink)
        o_ref[g * HEAD_DIM:(g + 1) * HEAD_DIM, :] = o.astype(BF16)


def _local_attention(sink, qt, k, vt, dims, tq):
    nb, s, n_ctx = dims["batch"], dims["seq"], dims["n_ctx"]
    nq = s // tq
    w = WINDOW
    per_b_w = s // w
    r = tq // w
    ctx_blk0 = (nb * s) // n_ctx

    def prev_idx(b, qi):
        return b * per_b_w + jnp.maximum(qi * r - 1, 0)

    def next_idx(b, qi):
        return b * per_b_w + jnp.minimum(qi * r + r, per_b_w - 1)

    return pl.pallas_call(
        _local_kernel,
        grid=(nb, KV_HEADS, nq),
        in_specs=[pl.BlockSpec(memory_space=pltpu.SMEM),
                  pl.BlockSpec((GROUP, LANES, tq), lambda b, kv, qi: (kv, 0, b * nq + qi)),
                  pl.BlockSpec((None, w, LANES), lambda b, kv, qi: (kv, prev_idx(b, qi), 0)),
                  pl.BlockSpec((None, tq, LANES), lambda b, kv, qi: (kv, b * nq + qi, 0)),
                  pl.BlockSpec((None, w, LANES), lambda b, kv, qi: (kv, next_idx(b, qi), 0)),
                  pl.BlockSpec((None, n_ctx, LANES), lambda b, kv, qi: (kv, ctx_blk0 + b, 0)),
                  pl.BlockSpec((None, HEAD_DIM, w), lambda b, kv, qi: (kv, 0, prev_idx(b, qi))),
                  pl.BlockSpec((None, HEAD_DIM, tq), lambda b, kv, qi: (kv, 0, b * nq + qi)),
                  pl.BlockSpec((None, HEAD_DIM, w), lambda b, kv, qi: (kv, 0, next_idx(b, qi))),
                  pl.BlockSpec((None, HEAD_DIM, n_ctx), lambda b, kv, qi: (kv, 0, ctx_blk0 + b))],
        out_specs=pl.BlockSpec((GROUP * HEAD_DIM, tq), lambda b, kv, qi: (kv, b * nq + qi)),
        out_shape=jax.ShapeDtypeStruct((N_HEADS_MIXER * HEAD_DIM, nb * s), BF16),
        compiler_params=pltpu.CompilerParams(
            dimension_semantics=("arbitrary", "arbitrary", "arbitrary"), vmem_limit_bytes=VMEM_LIMIT),
        name="local_attention",
    )(sink, qt, k, k, k, k, vt, vt, vt, vt)


def _ctx_kernel(sink_ref, qg_ref, kg_ref, vg_ref, ql_ref, kl_ref, vl_ref, o_ref):
    tq = qg_ref.shape[2]
    for mixer, (q_ref, k_ref, v_ref) in enumerate(((qg_ref, kg_ref, vg_ref), (ql_ref, kl_ref, vl_ref))):
        for kv in range(KV_HEADS):
            k_all = k_ref[kv]
            v1 = _with_ones(v_ref[kv])
            for g in range(GROUP):
                hq = kv * GROUP + g
                sink = jnp.full((1, tq), sink_ref[hq], F32) if mixer == 1 else None
                o = _single_pass_attend(k_all, v1, q_ref[hq], None, sink, jnp.exp2 if mixer == 0 else jnp.exp)
                row = (mixer * N_HEADS_MIXER + hq) * HEAD_DIM
                o_ref[row:row + HEAD_DIM, :] = o.astype(BF16)


def _ctx_attention(sink, qgt, kg, vgt, qlt, kl, vlt, dims):
    nb, s, n_ctx = dims["batch"], dims["seq"], dims["n_ctx"]
    ctx_blk0 = (nb * s) // n_ctx
    q_spec = pl.BlockSpec((N_HEADS_MIXER, LANES, n_ctx), lambda b: (0, 0, ctx_blk0 + b))
    k_spec = pl.BlockSpec((KV_HEADS, n_ctx, LANES), lambda b: (0, ctx_blk0 + b, 0))
    v_spec = pl.BlockSpec((KV_HEADS, HEAD_DIM, n_ctx), lambda b: (0, 0, ctx_blk0 + b))
    rows = 2 * N_HEADS_MIXER * HEAD_DIM
    return pl.pallas_call(
        _ctx_kernel,
        grid=(nb,),
        in_specs=[pl.BlockSpec(memory_space=pltpu.SMEM), q_spec, k_spec, v_spec, q_spec, k_spec, v_spec],
        out_specs=pl.BlockSpec((rows, n_ctx), lambda b: (0, b)),
        out_shape=jax.ShapeDtypeStruct((rows, nb * n_ctx), BF16),
        compiler_params=pltpu.CompilerParams(dimension_semantics=("arbitrary",), vmem_limit_bytes=VMEM_LIMIT),
        name="ctx_attention",
    )(sink, qgt, kg, vgt, qlt, kl, vlt)


def _route(sel, scores):
    n = EXPERTS_PER_GROUP
    group_score = []
    for g in range(N_GROUPS):
        r = sel[g * n:(g + 1) * n]
        best = None
        for a in range(n):
            for b in range(a + 1, n):
                pair = r[a] + r[b]
                best = pair if best is None else jnp.maximum(best, pair)
        group_score.append(best)
    gates = [None] * N_EXPERTS
    picked = []
    for g in range(N_GROUPS):
        chosen = None
        for h in range(N_GROUPS):
            if h == g:
                continue
            c = (group_score[g] > group_score[h]) if h < g else (group_score[g] >= group_score[h])
            chosen = c if chosen is None else (chosen & c)
        for a in range(n):
            ea = g * n + a
            rank = jnp.zeros_like(sel[ea])
            for b in range(n):
                if b == a:
                    continue
                eb = g * n + b
                ahead = (sel[eb] >= sel[ea]) if b < a else (sel[eb] > sel[ea])
                rank = rank + jnp.where(ahead, 1.0, 0.0)
            picked.append(chosen & (rank < 2.0))
    den = None
    for e in range(N_EXPERTS):
        w = jnp.where(picked[e], scores[e], 0.0)
        gates[e] = w
        den = w if den is None else den + w
    cls = jnp.zeros_like(den)
    for g in range(N_GROUPS):
        for code, (a, b) in enumerate(EXPERT_PAIRS):
            both = picked[g * n + a] & picked[g * n + b]
            cls = cls + jnp.where(both, float(g * len(EXPERT_PAIRS) + code), 0.0)
    return [w / den for w in gates], cls


def _post_kernel(yg_ref, yl_ref, yc_ref, x_ref, mod_ref, wout_ref, g_ref, rwt_ref, rb_ref,
                 x1_ref, h2x_ref, meta_ref, *, n_lat):
    i = pl.program_id(0)
    d = x_ref.shape[1]
    tm = x_ref.shape[0]
    yt_lat = jnp.concatenate([yg_ref[...], yl_ref[...]], axis=0)
    is_lat = jnp.full(yt_lat.shape, i, jnp.int32) < n_lat
    yt = jnp.where(is_lat, yt_lat, yc_ref[...])
    a = lax.dot_general(yt, wout_ref[...], (((0,), (0,)), ((), ())), preferred_element_type=F32)
    gate1 = mod_ref[:, 2 * d:3 * d]
    shift2 = mod_ref[:, 3 * d:4 * d]
    scale2 = mod_ref[:, 4 * d:5 * d]
    x1 = x_ref[...] + gate1 * a
    x1_ref[...] = x1
    h2 = _rms_rows(x1, g_ref[...]) * (1.0 + scale2) + shift2
    logits_t = lax.dot_general(rwt_ref[...], h2.astype(BF16), (((1,), (1,)), ((), ())),
                               preferred_element_type=F32)
    scores_t = _sigmoid(logits_t)
    sel_t = scores_t + rb_ref[...]
    gates, cls = _route([sel_t[e:e + 1] for e in range(N_EXPERTS)], [scores_t[e:e + 1] for e in range(N_EXPERTS)])
    gates_t = jnp.concatenate(gates + [jnp.zeros((LANES - N_EXPERTS, tm), F32)], axis=0)
    h2x_ref[:, 0:d] = h2
    h2x_ref[:, d:d + LANES] = gates_t.T
    meta_ref[...] = jnp.concatenate([cls, jnp.zeros((7, tm), F32)], axis=0)


def _post_attention(ygt, ylt, yct, x_all, mod, l, w_out_b, ffn_g, rw_t, rb_col, dims):
    ntok, d = x_all.shape
    tm = TOKEN_TILE
    n_lat, per_b, nb = dims["n_lat_tiles"], dims["tiles_per_batch"], dims["batch"]
    rows = ygt.shape[0]

    def mod_idx(i):
        return (l, jnp.minimum(i // per_b, nb), 0, 0)

    def lat_idx(i):
        return (0, jnp.minimum(i, n_lat - 1))

    return pl.pallas_call(
        functools.partial(_post_kernel, n_lat=n_lat),
        grid=(ntok // tm,),
        in_specs=[pl.BlockSpec((rows, tm), lat_idx),
                  pl.BlockSpec((rows, tm), lat_idx),
                  pl.BlockSpec((2 * rows, tm), lambda i: (0, 0)),
                  pl.BlockSpec((tm, d), lambda i: (i, 0)),
                  pl.BlockSpec((None, None, 1, mod.shape[-1]), mod_idx),
                  pl.BlockSpec(w_out_b.shape, lambda i: (0, 0)),
                  pl.BlockSpec((1, d), lambda i: (0, 0)),
                  pl.BlockSpec(rw_t.shape, lambda i: (0, 0)),
                  pl.BlockSpec(rb_col.shape, lambda i: (0, 0))],
        out_specs=[pl.BlockSpec((tm, d), lambda i: (i, 0)),
                   pl.BlockSpec((tm, d + LANES), lambda i: (i, 0)),
                   pl.BlockSpec((8, tm), lambda i: (0, i))],
        out_shape=[jax.ShapeDtypeStruct((ntok, d), F32),
                   jax.ShapeDtypeStruct((ntok, d + LANES), F32),
                   jax.ShapeDtypeStruct((8, ntok), F32)],
        compiler_params=pltpu.CompilerParams(dimension_semantics=("arbitrary",), vmem_limit_bytes=VMEM_LIMIT),
        name="post_attention",
    )(ygt, ylt, yct, x_all, mod, w_out_b, ffn_g, rw_t, rb_col)


def _sort_plan(cls, tm):
    ntok = cls.shape[0]
    p_tiles = -(-ntok // tm) + N_CLASSES
    order = jnp.argsort(cls, stable=True).astype(jnp.int32)
    counts = jnp.sum((cls[:, None] == jnp.arange(N_CLASSES, dtype=jnp.int32)[None, :]).astype(jnp.int32), axis=0)
    tiles_per = (counts + tm - 1) // tm
    tile_end = jnp.cumsum(tiles_per)
    start = jnp.cumsum(counts) - counts
    tile_cls = jnp.minimum(jnp.searchsorted(tile_end, jnp.arange(p_tiles, dtype=jnp.int32), side="right"),
                           N_CLASSES - 1).astype(jnp.int32)
    pos = jnp.arange(p_tiles * tm, dtype=jnp.int32)
    c = tile_cls[pos // tm]
    r = pos - (tile_end[c] - tiles_per[c]) * tm
    valid = (r < counts[c]) & (pos // tm < tile_end[N_CLASSES - 1])
    token = order[jnp.clip(start[c] + r, 0, ntok - 1)]
    src = jnp.where(valid, token, 0).astype(jnp.int32)
    dst = jnp.where(valid, token, ntok + pos % tm).astype(jnp.int32)
    pair = jnp.asarray(EXPERT_PAIRS, jnp.int32)[tile_cls % len(EXPERT_PAIRS)]
    base = (tile_cls // len(EXPERT_PAIRS)) * EXPERTS_PER_GROUP
    return dict(src=src.reshape(p_tiles, 1, tm), dst=dst.reshape(p_tiles, 1, tm), ea=base + pair[:, 0],
                eb=base + pair[:, 1], n_used=tile_end[N_CLASSES - 1:].astype(jnp.int32), p_tiles=p_tiles)


def _row_copy_all(idx_ref, n_rows, make_copy):
    def body(r, carry):
        make_copy(idx_ref[0, 0, r], r).start()
        return carry
    lax.fori_loop(0, n_rows, body, 0, unroll=8)


def _gather_kernel(src_ref, h_hbm, o_ref, sem):
    n_rows = o_ref.shape[0]
    _row_copy_all(src_ref, n_rows,
                  lambda tok, r: pltpu.make_async_copy(h_hbm.at[pl.ds(tok, 1)], o_ref.at[pl.ds(r, 1)], sem))
    pltpu.make_async_copy(h_hbm.at[pl.ds(0, n_rows)], o_ref, sem).wait()


def _gather_rows(src, h2x, tm):
    p_tiles = src.shape[0]
    width = h2x.shape[1]
    return pl.pallas_call(
        _gather_kernel,
        grid=(p_tiles,),
        in_specs=[pl.BlockSpec((1, 1, tm), lambda t: (t, 0, 0), memory_space=pltpu.SMEM),
                  pl.BlockSpec(memory_space=pl.ANY)],
        out_specs=pl.BlockSpec((tm, width), lambda t: (t, 0)),
        out_shape=jax.ShapeDtypeStruct((p_tiles * tm, width), F32),
        scratch_shapes=[pltpu.SemaphoreType.DMA(())],
        compiler_params=pltpu.CompilerParams(dimension_semantics=("arbitrary",), vmem_limit_bytes=VMEM_LIMIT),
        name="moe_gather",
    )(src, h2x)


def _scatter_kernel(dst_ref, y_ref, o_hbm, sem):
    n_rows = y_ref.shape[0]
    _row_copy_all(dst_ref, n_rows,
                  lambda tok, r: pltpu.make_async_copy(y_ref.at[pl.ds(r, 1)], o_hbm.at[pl.ds(tok, 1)], sem))
    pltpu.make_async_copy(y_ref, o_hbm.at[pl.ds(0, n_rows)], sem).wait()


def _scatter_rows(dst, ys, ntok, tm):
    p_tiles = dst.shape[0]
    d = ys.shape[1]
    return pl.pallas_call(
        _scatter_kernel,
        grid=(p_tiles,),
        in_specs=[pl.BlockSpec((1, 1, tm), lambda t: (t, 0, 0), memory_space=pltpu.SMEM),
                  pl.BlockSpec((tm, d), lambda t: (t, 0))],
        out_specs=pl.BlockSpec(memory_space=pl.ANY),
        out_shape=jax.ShapeDtypeStruct((ntok + tm, d), F32),
        scratch_shapes=[pltpu.SemaphoreType.DMA(())],
        compiler_params=pltpu.CompilerParams(dimension_semantics=("arbitrary",), vmem_limit_bytes=VMEM_LIMIT),
        name="moe_scatter",
    )(dst, ys)


def _expert_pair_kernel(ea_ref, eb_ref, nu_ref, x_ref, wga_ref, wua_ref, wda_ref, wgb_ref, wub_ref, wdb_ref, o_ref):
    t = pl.program_id(0)
    d = o_ref.shape[1]

    @pl.when(t < nu_ref[0])
    def _():
        h = x_ref[:, 0:d].astype(BF16)
        gates = x_ref[:, d:d + LANES]
        lane = lax.broadcasted_iota(jnp.int32, gates.shape, 1)
        y = None
        for e_ref, wg_ref, wu_ref, wd_ref in ((ea_ref, wga_ref, wua_ref, wda_ref), (eb_ref, wgb_ref, wub_ref, wdb_ref)):
            gcol = jnp.sum(jnp.where(lane == e_ref[t], gates, 0.0), axis=1, keepdims=True)
            a = _dot(h, wg_ref[0, 0])
            u = _dot(h, wu_ref[0, 0])
            he = ((a * _sigmoid(a)) * u * gcol).astype(BF16)
            part = _dot(he, wd_ref[0, 0])
            y = part if y is None else y + part
        o_ref[...] = y


def _expert_pairs(plan, hs, wg_b, wu_b, wd_b, l, tm):
    d = wd_b.shape[-1]
    f = wg_b.shape[-1]
    width = hs.shape[1]

    def w_spec(shape, which):
        return pl.BlockSpec((1, 1) + shape, lambda t, ea, eb, nu: (l, (ea, eb)[which][t], 0, 0))

    return pl.pallas_call(
        _expert_pair_kernel,
        grid_spec=pltpu.PrefetchScalarGridSpec(
            num_scalar_prefetch=3,
            grid=(plan["p_tiles"],),
            in_specs=[pl.BlockSpec((tm, width), lambda t, ea, eb, nu: (t, 0)),
                      w_spec((d, f), 0), w_spec((d, f), 0), w_spec((f, d), 0),
                      w_spec((d, f), 1), w_spec((d, f), 1), w_spec((f, d), 1)],
            out_specs=pl.BlockSpec((tm, d), lambda t, ea, eb, nu: (t, 0))),
        out_shape=jax.ShapeDtypeStruct((plan["p_tiles"] * tm, d), F32),
        compiler_params=pltpu.CompilerParams(dimension_semantics=("arbitrary",), vmem_limit_bytes=VMEM_LIMIT),
        name="moe_expert_pairs",
    )(plan["ea"], plan["eb"], plan["n_used"], hs, wg_b, wu_b, wd_b, wg_b, wu_b, wd_b)


def _moe(h2x, meta, wg_b, wu_b, wd_b, l):
    ntok = h2x.shape[0]
    tm = MOE_TILE
    plan = _sort_plan(meta[0].astype(jnp.int32), tm)
    hs = _gather_rows(plan["src"], h2x, tm)
    ys = _expert_pairs(plan, hs, wg_b, wu_b, wd_b, l, tm)
    return _scatter_rows(plan["dst"], ys, ntok, tm)


def _combine_kernel(x1_ref, yf_ref, mod_ref, o_ref):
    d = x1_ref.shape[1]
    o_ref[...] = x1_ref[...] + mod_ref[:, 5 * d:6 * d] * yf_ref[...]


def _combine(x1, yf, mod, l, n_rows, dims):
    d = x1.shape[1]
    tm = TOKEN_TILE
    per_b, nb = dims["tiles_per_batch"], dims["batch"]
    return pl.pallas_call(
        _combine_kernel,
        grid=(n_rows // tm,),
        in_specs=[pl.BlockSpec((tm, d), lambda i: (i, 0)),
                  pl.BlockSpec((tm, d), lambda i: (i, 0)),
                  pl.BlockSpec((None, None, 1, mod.shape[-1]), lambda i: (l, jnp.minimum(i // per_b, nb), 0, 0))],
        out_specs=pl.BlockSpec((tm, d), lambda i: (i, 0)),
        out_shape=jax.ShapeDtypeStruct((n_rows, d), F32),
        compiler_params=pltpu.CompilerParams(dimension_semantics=("arbitrary",), vmem_limit_bytes=VMEM_LIMIT),
        name="moe_combine",
    )(x1, yf, mod)


def _rope_tables(s, pad):
    rows = s // GRID_W
    row = jnp.repeat(jnp.arange(rows, dtype=jnp.int32), GRID_W).astype(F32)
    col = jnp.tile(jnp.arange(GRID_W, dtype=jnp.int32), rows).astype(F32)
    inv = ROPE_THETA ** (-jnp.arange(N_FREQ, dtype=F32) / N_FREQ)
    ang_r, ang_c = row[:, None] * inv, col[:, None] * inv
    cr, sr, cc, sc = jnp.cos(ang_r), jnp.sin(ang_r), jnp.cos(ang_c), jnp.sin(ang_c)
    one, zero = jnp.ones((pad, N_FREQ), F32), jnp.zeros((pad, N_FREQ), F32)
    cs_t = jnp.concatenate([jnp.concatenate([cr, sr, cc, sc], axis=1),
                            jnp.concatenate([one, zero, one, zero], axis=1)], axis=0).T
    cpat = jnp.concatenate([cr, cr, cc, cc], axis=1)
    spat = jnp.concatenate([-sr, sr, -sc, sc], axis=1)
    ident = jnp.concatenate([jnp.ones((pad, 2 * HEAD_DIM), F32), jnp.zeros((pad, 2 * HEAD_DIM), F32)], axis=1)
    cs_nat = jnp.concatenate([jnp.concatenate([cpat, cpat, spat, spat], axis=1), ident], axis=0)
    return cs_nat, cs_t


def kernel(x, c, ctx, c_ctx, ada_w, ada_b, attn_norm_g, ffn_norm_g, w_in, qn_global, kn_global, qn_local, kn_local,
           sink_logit, w_out, router_w, router_b, w_gate, w_up, w_down):
    nb, s, d = x.shape
    n_ctx = ctx.shape[1]
    depth = ada_w.shape[0]
    tm = TOKEN_TILE
    assert s % tm == 0 and nb * n_ctx == tm and s % GRID_W == 0 and nb + 1 <= MOD_ROWS
    dims = dict(batch=nb, seq=s, n_ctx=n_ctx, tiles_per_batch=s // tm, n_lat_tiles=nb * s // tm)

    cvec = jnp.concatenate([c, c_ctx[None, :], jnp.zeros((MOD_ROWS - nb - 1, d), F32)], axis=0)
    mod = _modulation(cvec, ada_w, ada_b)[:, :nb + 1].reshape(depth, nb + 1, 1, 6 * d)

    cs_nat, cs_t = _rope_tables(s, tm)
    head_of_lane = np.arange(LANES) // HEAD_DIM
    block_ones = jnp.asarray(head_of_lane[:, None] == head_of_lane[None, :], BF16)
    w_in_b = w_in.astype(BF16)
    w_out_b = w_out.astype(BF16)
    wg_b, wu_b, wd_b = w_gate.astype(BF16), w_up.astype(BF16), w_down.astype(BF16)
    rw_t = router_w.T.astype(BF16)
    rb_col = router_b.reshape(N_EXPERTS, 1).astype(F32)

    stream = (jnp.concatenate([x.reshape(nb * s, d), ctx.reshape(nb * n_ctx, d)], axis=0),)
    tq_g, tk_g = min(512, s), min(2048, s)
    tq_l = 256
    for l in range(depth):
        gq_cols = jnp.stack([qn_global[l], qn_local[l]]).reshape(2, HEAD_DIM, 1)
        gk_rows = jnp.stack([jnp.tile(kn_global[l], KV_HEADS), jnp.tile(kn_local[l], KV_HEADS)]).reshape(2, 1, LANES)
        outs = _pre_attention(
            stream, mod, l, attn_norm_g[l].reshape(1, d), w_in_b[l], gq_cols, gk_rows, block_ones, cs_nat, cs_t, dims)
        qgt, kg, vgt, qlt, kl, vlt = outs[:6]
        x_all = outs[6] if len(outs) > 6 else stream[0]
        ygt = _global_attention(qgt, kg, vgt, dims, tq_g, tk_g)
        ylt = _local_attention(sink_logit[l], qlt, kl, vlt, dims, tq_l)
        yct = _ctx_attention(sink_logit[l], qgt, kg, vgt, qlt, kl, vlt, dims)
        x1, h2x, meta = _post_attention(ygt, ylt, yct, x_all, mod, l, w_out_b[l], ffn_norm_g[l].reshape(1, d),
                                        rw_t, rb_col, dims)
        stream = (x1, _moe(h2x, meta, wg_b, wu_b, wd_b, l))
    return _combine(stream[0], stream[1], mod, depth - 1, nb * s, dims).reshape(nb, s, d)
```

```python
import functools
import math

import numpy as np
import jax
import jax.numpy as jnp
from jax import lax
from jax.experimental import pallas as pl
from jax.experimental.pallas import tpu as pltpu

F32 = jnp.float32
BF16 = jnp.bfloat16
FP8 = jnp.float8_e4m3fn

HEAD_DIM = 64
GRID_W = 64
WINDOW = 128
N_HEADS_MIXER = 8
KV_HEADS = 2
GROUP = N_HEADS_MIXER // KV_HEADS
ROPE_THETA = 10000.0
N_EXPERTS = 16
N_GROUPS = 4
EXPERTS_PER_GROUP = N_EXPERTS // N_GROUPS
EXPERT_PAIRS = tuple((a, b) for a in range(EXPERTS_PER_GROUP) for b in range(a + 1, EXPERTS_PER_GROUP))
N_CLASSES = N_GROUPS * len(EXPERT_PAIRS)
NORM_EPS = 1e-6
ATTN_SCALE = 1.0 / math.sqrt(HEAD_DIM)
LOG2_E = 1.4426950408889634
Q_SCALES = (ATTN_SCALE * LOG2_E, ATTN_SCALE)
NEG_INF = -1e30
N_FREQ = HEAD_DIM // 4

LANES = 128
TOKEN_TILE = 512
MOE_TILE = 256
QK_DEPTH = 256
R_ROW = 3 * HEAD_DIM
R_TILE = 32
R_MAX = 256.0
P_SHIFT = 3.0
P_LIMIT = 256.0
ONES_ROWS = 16
V_ROWS = HEAD_DIM + ONES_ROWS
VMEM_LIMIT = 48 * 1024 * 1024
MOD_ROWS = 16


def _dot(a, b):
    return jnp.dot(a, b, preferred_element_type=F32)


def _sigmoid(x):
    return 1.0 / (1.0 + jnp.exp(-x))


def _mod_kernel(c_ref, w_ref, b_ref, o_ref):
    c = c_ref[...]
    s = (c * _sigmoid(c)).astype(BF16)
    o_ref[0] = _dot(s, w_ref[0].astype(BF16)) + b_ref[0]


def _modulation(cvec, ada_w, ada_b):
    depth, d, d6 = ada_w.shape
    col = 1536
    assert d6 % col == 0
    return pl.pallas_call(
        _mod_kernel,
        grid=(depth, d6 // col),
        in_specs=[pl.BlockSpec((MOD_ROWS, d), lambda l, j: (0, 0)),
                  pl.BlockSpec((1, d, col), lambda l, j: (l, 0, j)),
                  pl.BlockSpec((1, 1, col), lambda l, j: (l, 0, j))],
        out_specs=pl.BlockSpec((1, MOD_ROWS, col), lambda l, j: (l, 0, j)),
        out_shape=jax.ShapeDtypeStruct((depth, MOD_ROWS, d6), F32),
        compiler_params=pltpu.CompilerParams(dimension_semantics=("arbitrary", "arbitrary"),
                                             vmem_limit_bytes=VMEM_LIMIT),
        name="modulation",
    )(cvec, ada_w, ada_b.reshape(depth, 1, d6))


def _rms_rows(x, gain):
    ms = jnp.mean(x * x, axis=-1, keepdims=True)
    return x * lax.rsqrt(ms + NORM_EPS) * gain


def _rope_t(t, cs_t):
    n = N_FREQ
    cr, sr, cc, sc = cs_t[0:n], cs_t[n:2 * n], cs_t[2 * n:3 * n], cs_t[3 * n:4 * n]
    x1r, x2r, x1c, x2c = t[0:n], t[n:2 * n], t[2 * n:3 * n], t[3 * n:4 * n]
    return jnp.concatenate([x1r * cr - x2r * sr, x2r * cr + x1r * sr,
                            x1c * cc - x2c * sc, x2c * cc + x1c * sc], axis=0)


def _q_heads_t(y_pair, gain_col, cs_t, q_scale):
    t = y_pair.T
    outs = []
    for j in range(2):
        th = t[j * HEAD_DIM:(j + 1) * HEAD_DIM]
        ms = jnp.mean(th * th, axis=0, keepdims=True)
        thn = th * lax.rsqrt(ms + NORM_EPS) * gain_col
        outs.append(_rope_t(thn, cs_t) * q_scale)
    return outs


def _split_fp8(x):
    hi = x.astype(FP8).astype(F32)
    lo = (x - hi).astype(FP8).astype(F32)
    return hi, lo


def _k_heads(kc, block_ones, gain_row, cs):
    sq = kc * kc
    hi = sq.astype(BF16)
    lo = (sq - hi.astype(F32)).astype(BF16)
    ss = _dot(hi, block_ones) + _dot(lo, block_ones)
    kn = kc * lax.rsqrt(ss * (1.0 / HEAD_DIM) + NORM_EPS) * gain_row
    lane = lax.broadcasted_iota(jnp.int32, kn.shape, 1)
    first_half = (lane % (2 * N_FREQ)) < N_FREQ
    swapped = jnp.where(first_half, pltpu.roll(kn, LANES - N_FREQ, 1), pltpu.roll(kn, N_FREQ, 1))
    return kn * cs[:, :LANES] + swapped * cs[:, LANES:]


def _pre_kernel(*refs, combine):
    if combine:
        x1_ref, yf_ref, modp_ref = refs[:3]
        refs = refs[3:]
        xo_ref = refs[-1]
        refs = refs[:-1]
        d = x1_ref.shape[1]
        x = x1_ref[...] + modp_ref[:, 5 * d:6 * d] * yf_ref[...]
        xo_ref[...] = x
    else:
        x = refs[0][...]
        refs = refs[1:]
    (mod_ref, g_ref, w_ref, gq_ref, gk_ref, bones_ref, cs_ref, cst_ref,
     qgt_ref, kg_ref, vgt_ref, qlt_ref, kl_ref, vlt_ref) = refs
    tm, d = x.shape
    shift = mod_ref[:, 0:d]
    scale = mod_ref[:, d:2 * d]
    h = _rms_rows(x, g_ref[...]) * (1.0 + scale) + shift
    y = _dot(h.astype(BF16), w_ref[...])
    cs_t = cst_ref[...]
    cs = cs_ref[...]
    zeros = jnp.zeros((HEAD_DIM, tm), BF16)
    qw = N_HEADS_MIXER * HEAD_DIM
    kw = KV_HEADS * HEAD_DIM
    for mixer, (qt_ref, k_ref, vt_ref) in enumerate(((qgt_ref, kg_ref, vgt_ref), (qlt_ref, kl_ref, vlt_ref))):
        base = mixer * (qw + 2 * kw)
        gain_col = gq_ref[mixer]
        q_scale = Q_SCALES[mixer]
        split = mixer == 0
        for c in range(N_HEADS_MIXER // 2):
            pair = _q_heads_t(y[:, base + c * LANES: base + (c + 1) * LANES], gain_col, cs_t, q_scale)
            for j in range(2):
                if split:
                    hi, lo = _split_fp8(pair[j])
                    qt_ref[2 * c + j] = jnp.concatenate([hi, hi, lo, jnp.zeros_like(hi)], axis=0).astype(FP8)
                else:
                    qt_ref[2 * c + j] = jnp.concatenate([pair[j].astype(BF16), zeros], axis=0)
        kc = y[:, base + qw: base + qw + kw]
        kboth = _k_heads(kc, bones_ref[...], gk_ref[mixer], cs)
        lane = lax.broadcasted_iota(jnp.int32, kboth.shape, 1)
        low = lane < HEAD_DIM
        ones_col = jnp.where(lane == HEAD_DIM, 1.0, 0.0)
        if split:
            hi, lo = _split_fp8(kboth)
            hi_sw, lo_sw = pltpu.roll(hi, HEAD_DIM, 1), pltpu.roll(lo, HEAD_DIM, 1)
            k_ref[0] = jnp.concatenate([jnp.where(low, hi, lo_sw), jnp.where(low, hi, ones_col)], axis=1).astype(FP8)
            k_ref[1] = jnp.concatenate([jnp.where(low, hi_sw, lo), jnp.where(low, hi_sw, ones_col)], axis=1).astype(FP8)
        else:
            k_ref[0] = jnp.where(low, kboth, ones_col).astype(BF16)
            k_ref[1] = jnp.where(low, pltpu.roll(kboth, HEAD_DIM, 1), ones_col).astype(BF16)
        vc = y[:, base + qw + kw: base + qw + 2 * kw]
        vt = vc.T.astype(BF16)
        for kv in range(KV_HEADS):
            vt_ref[kv] = vt[kv * HEAD_DIM:(kv + 1) * HEAD_DIM]


def _pre_attention(stream, mod, l, attn_g, w_in_b, gq_cols, gk_rows, block_ones, cs_nat, cs_t, dims):
    combine = len(stream) == 2
    ntok, d = stream[0].shape
    tm = TOKEN_TILE
    n_lat, per_b, nb = dims["n_lat_tiles"], dims["tiles_per_batch"], dims["batch"]
    n_tiles = ntok // tm

    def mod_idx(i):
        return (l, jnp.minimum(i // per_b, nb), 0, 0)

    def tab_idx(i):
        return jnp.where(i < n_lat, i % per_b, per_b)

    qt_shape = jax.ShapeDtypeStruct((N_HEADS_MIXER, LANES, ntok), BF16)
    k_shape = jax.ShapeDtypeStruct((KV_HEADS, ntok, LANES), BF16)
    vt_shape = jax.ShapeDtypeStruct((KV_HEADS, HEAD_DIM, ntok), BF16)
    qt_spec = pl.BlockSpec((N_HEADS_MIXER, LANES, tm), lambda i: (0, 0, i))
    k_spec = pl.BlockSpec((KV_HEADS, tm, LANES), lambda i: (0, i, 0))
    vt_spec = pl.BlockSpec((KV_HEADS, HEAD_DIM, tm), lambda i: (0, 0, i))
    qt8_shape = jax.ShapeDtypeStruct((N_HEADS_MIXER, QK_DEPTH, ntok), FP8)
    k8_shape = jax.ShapeDtypeStruct((KV_HEADS, ntok, QK_DEPTH), FP8)
    qt8_spec = pl.BlockSpec((N_HEADS_MIXER, QK_DEPTH, tm), lambda i: (0, 0, i))
    k8_spec = pl.BlockSpec((KV_HEADS, tm, QK_DEPTH), lambda i: (0, i, 0))
    row_spec =pl.BlockSpec((tm, d), lambda i: (i, 0))
    mod_spec = pl.BlockSpec((None, None, 1, mod.shape[-1]), mod_idx)
    if combine:
        modp_spec = pl.BlockSpec((None, None, 1, mod.shape[-1]), lambda i: (l - 1,) + mod_idx(i)[1:])
        stream_specs, stream_args = [row_spec, row_spec, modp_spec], [stream[0], stream[1], mod]
        extra_specs, extra_shapes = [row_spec], [jax.ShapeDtypeStruct((ntok, d), F32)]
    else:
        stream_specs, stream_args, extra_specs, extra_shapes = [row_spec], [stream[0]], [], []
    return pl.pallas_call(
        functools.partial(_pre_kernel, combine=combine),
        grid=(n_tiles,),
        in_specs=stream_specs + [
                  mod_spec,
                  pl.BlockSpec((1, d), lambda i: (0, 0)),
                  pl.BlockSpec(w_in_b.shape, lambda i: (0, 0)),
                  pl.BlockSpec(gq_cols.shape, lambda i: (0, 0, 0)),
                  pl.BlockSpec(gk_rows.shape, lambda i: (0, 0, 0)),
                  pl.BlockSpec(block_ones.shape, lambda i: (0, 0)),
                  pl.BlockSpec((tm, 2 * LANES), lambda i: (tab_idx(i), 0)),
                  pl.BlockSpec((HEAD_DIM, tm), lambda i: (0, tab_idx(i)))],
        out_specs=[qt8_spec, k8_spec, vt_spec, qt_spec, k_spec, vt_spec] + extra_specs,
        out_shape=[qt8_shape, k8_shape, vt_shape, qt_shape, k_shape, vt_shape] + extra_shapes,
        compiler_params=pltpu.CompilerParams(dimension_semantics=("arbitrary",), vmem_limit_bytes=VMEM_LIMIT),
        name="pre_attention",
    )(*stream_args, mod, attn_g, w_in_b, gq_cols, gk_rows, block_ones, cs_nat, cs_t)


def _with_ones(vt):
    return jnp.concatenate([vt, jnp.ones((ONES_ROWS, vt.shape[1]), BF16)], axis=0)


def _round_ref(x):
    inside = jnp.abs(x) <= R_MAX
    return jnp.where(inside, jnp.clip(x, -R_MAX, R_MAX).astype(FP8).astype(F32), x)


def _flash_kernel(q_ref, k_ref, v_ref, kc_ref, vc_ref, o_ref, r_scr, acc_scr, new_scr, pmax_scr):
    ki = pl.program_id(3)
    tq = q_ref.shape[2]
    zpad = jnp.zeros((QK_DEPTH - R_ROW - R_TILE, tq), FP8)

    def q_ext(g, r_row):
        neg = jnp.concatenate([-r_row, jnp.zeros((R_TILE - 1, tq), F32)], axis=0).astype(FP8)
        return jnp.concatenate([q_ref[g, 0:R_ROW], neg, zpad], axis=0)

    def two_pass_update(g, k, v1, first):
        r_old = jnp.zeros((1, tq), F32) if first else r_scr[g]
        st = _dot(k, q_ref[g]) - r_old
        top = jnp.max(st, axis=0, keepdims=True) - P_SHIFT
        delta = _round_ref(r_old + (top if first else jnp.maximum(top, 0.0))) - r_old
        p = jnp.exp2(st - delta).astype(BF16)
        pv = _dot(v1, p)
        acc_scr[g] = pv if first else acc_scr[g] * jnp.exp2(-delta) + pv
        r_scr[g] = r_old + delta

    @pl.when(ki == 0)
    def _():
        vc1 = _with_ones(vc_ref[...])
        kc = kc_ref[...]
        for g in range(GROUP):
            two_pass_update(g, kc, vc1, True)

    v1 = _with_ones(v_ref[...])
    v8 = v1.astype(FP8)
    k = k_ref[...]
    for g in range(GROUP):
        st = _dot(k, q_ext(g, r_scr[g]))
        p = jnp.exp2(st.astype(BF16))
        pmax_scr[g] = jnp.max(p, axis=0, keepdims=True).astype(F32)
        new_scr[g] = acc_scr[g] + _dot(v8, p.astype(FP8))
    clean = (jnp.max(pmax_scr[...]) <= P_LIMIT) & (jnp.max(jnp.abs(r_scr[...])) <= R_MAX)

    @pl.when(clean)
    def _():
        for g in range(GROUP):
            r_old = r_scr[g]
            r_new = _round_ref(r_old + jnp.maximum(jnp.log2(pmax_scr[g]) - P_SHIFT, 0.0))
            acc_scr[g] = new_scr[g] * jnp.exp2(r_old - r_new)
            r_scr[g] = r_new

    @pl.when(jnp.logical_not(clean))
    def _():
        for g in range(GROUP):
            two_pass_update(g, k, v1, False)

    @pl.when(ki == pl.num_programs(3) - 1)
    def _():
        for g in range(GROUP):
            a = acc_scr[g]
            o_ref[g * HEAD_DIM:(g + 1) * HEAD_DIM, :] = (a[0:HEAD_DIM] / a[HEAD_DIM:HEAD_DIM + 1]).astype(BF16)


def _global_attention(qt, k, vt, dims, tq, tk):
    nb, s, n_ctx = dims["batch"], dims["seq"], dims["n_ctx"]
    nq, nk = s // tq, s // tk
    ctx_blk0 = (nb * s) // n_ctx
    return pl.pallas_call(
        _flash_kernel,
        grid=(nb, KV_HEADS, nq, nk),
        in_specs=[pl.BlockSpec((GROUP, QK_DEPTH, tq), lambda b, kv, qi, ki: (kv, 0, b * nq + qi)),
                  pl.BlockSpec((None, tk, QK_DEPTH), lambda b, kv, qi, ki: (kv, b * nk + ki, 0)),
                  pl.BlockSpec((None, HEAD_DIM, tk), lambda b, kv, qi, ki: (kv, 0, b * nk + ki)),
                  pl.BlockSpec((None, n_ctx, QK_DEPTH), lambda b, kv, qi, ki: (kv, ctx_blk0 + b, 0)),
                  pl.BlockSpec((None, HEAD_DIM, n_ctx), lambda b, kv, qi, ki: (kv, 0, ctx_blk0 + b))],
        out_specs=pl.BlockSpec((GROUP * HEAD_DIM, tq), lambda b, kv, qi, ki: (kv, b * nq + qi)),
        out_shape=jax.ShapeDtypeStruct((N_HEADS_MIXER * HEAD_DIM, nb * s), BF16),
        scratch_shapes=[pltpu.VMEM((GROUP, 1, tq), F32), pltpu.VMEM((GROUP, V_ROWS, tq), F32),
                        pltpu.VMEM((GROUP, V_ROWS, tq), F32), pltpu.VMEM((GROUP, 1, tq), F32)],
        compiler_params=pltpu.CompilerParams(
            dimension_semantics=("arbitrary", "arbitrary", "arbitrary", "arbitrary"), vmem_limit_bytes=VMEM_LIMIT),
        name="global_attention",
    )(qt, k, vt, k, vt)


def _single_pass_attend(k_all, v1_all, q_t, mask, sink, exp=jnp.exp):
    st = _dot(k_all, q_t)
    if mask is not None:
        st = jnp.where(mask, st, NEG_INF)
    m = jnp.max(st, axis=0, keepdims=True)
    if sink is not None:
        m = jnp.maximum(m, sink)
    p = exp(st - m).astype(BF16)
    a = _dot(v1_all, p)
    den = a[HEAD_DIM:HEAD_DIM + 1]
    if sink is not None:
        den = den + exp(sink - m)
    return a[0:HEAD_DIM] / den


def _local_kernel(sink_ref, q_ref, kp_ref, km_ref, kn_ref, kc_ref, vp_ref, vm_ref, vn_ref, vc_ref, o_ref):
    kv = pl.program_id(1)
    qi = pl.program_id(2)
    nq = pl.num_programs(2)
    tq = q_ref.shape[2]
    n_ctx = kc_ref.shape[0]
    span = tq + 2 * WINDOW
    k_all = jnp.concatenate([kp_ref[...], km_ref[...], kn_ref[...], kc_ref[...]], axis=0)
    v1_all = _with_ones(jnp.concatenate([vp_ref[...], vm_ref[...], vn_ref[...], vc_ref[...]], axis=1))
    shape = (span + n_ctx, tq)
    j = lax.broadcasted_iota(jnp.int32, shape, 0)
    i = lax.broadcasted_iota(jnp.int32, shape, 1)
    band = jnp.abs(j - WINDOW - i) <= WINDOW
    qi_v = jnp.full(shape, qi, jnp.int32)
    in_seq = ((j >= WINDOW) | (qi_v > 0)) & ((j < WINDOW + tq) | (qi_v < nq - 1))
    mask = (band & in_seq & (j < span)) | (j >= span)
    for g in range(GROUP):
        sink = jnp.full((1, tq), sink_ref[kv * GROUP + g], F32)
        o = _single_pass_attend(k_all, v1_all, q_ref[g], mask, sink)
        o_ref[g * HEAD_DIM:(g + 1) * HEAD_DIM, :] = o.astype(BF16)


def _local_attention(sink, qt, k, vt, dims, tq):
    nb, s, n_ctx = dims["batch"], dims["seq"], dims["n_ctx"]
    nq = s // tq
    w = WINDOW
    per_b_w = s // w
    r = tq // w
    ctx_blk0 = (nb * s) // n_ctx

    def prev_idx(b, qi):
        return b * per_b_w + jnp.maximum(qi * r - 1, 0)

    def next_idx(b, qi):
        return b * per_b_w + jnp.minimum(qi * r + r, per_b_w - 1)

    return pl.pallas_call(
        _local_kernel,
        grid=(nb, KV_HEADS, nq),
        in_specs=[pl.BlockSpec(memory_space=pltpu.SMEM),
                  pl.BlockSpec((GROUP, LANES, tq), lambda b, kv, qi: (kv, 0, b * nq + qi)),
                  pl.BlockSpec((None, w, LANES), lambda b, kv, qi: (kv, prev_idx(b, qi), 0)),
                  pl.BlockSpec((None, tq, LANES), lambda b, kv, qi: (kv, b * nq + qi, 0)),
                  pl.BlockSpec((None, w, LANES), lambda b, kv, qi: (kv, next_idx(b, qi), 0)),
                  pl.BlockSpec((None, n_ctx, LANES), lambda b, kv, qi: (kv, ctx_blk0 + b, 0)),
                  pl.BlockSpec((None, HEAD_DIM, w), lambda b, kv, qi: (kv, 0, prev_idx(b, qi))),
                  pl.BlockSpec((None, HEAD_DIM, tq), lambda b, kv, qi: (kv, 0, b * nq + qi)),
                  pl.BlockSpec((None, HEAD_DIM, w), lambda b, kv, qi: (kv, 0, next_idx(b, qi))),
                  pl.BlockSpec((None, HEAD_DIM, n_ctx), lambda b, kv, qi: (kv, 0, ctx_blk0 + b))],
        out_specs=pl.BlockSpec((GROUP * HEAD_DIM, tq), lambda b, kv, qi: (kv, b * nq + qi)),
        out_shape=jax.ShapeDtypeStruct((N_HEADS_MIXER * HEAD_DIM, nb * s), BF16),
        compiler_params=pltpu.CompilerParams(
            dimension_semantics=("arbitrary", "arbitrary", "arbitrary"), vmem_limit_bytes=VMEM_LIMIT),
        name="local_attention",
    )(sink, qt, k, k, k, k, vt, vt, vt, vt)


def _ctx_kernel(sink_ref, qg_ref, kg_ref, vg_ref, ql_ref, kl_ref, vl_ref, o_ref):
    tq = qg_ref.shape[2]
    for mixer, (q_ref, k_ref, v_ref) in enumerate(((qg_ref, kg_ref, vg_ref), (ql_ref, kl_ref, vl_ref))):
        for kv in range(KV_HEADS):
            k_all = k_ref[kv]
            v1 = _with_ones(v_ref[kv])
            for g in range(GROUP):
                hq = kv * GROUP + g
                sink = jnp.full((1, tq), sink_ref[hq], F32) if mixer == 1 else None
                o = _single_pass_attend(k_all, v1, q_ref[hq], None, sink, jnp.exp2 if mixer == 0 else jnp.exp)
                row = (mixer * N_HEADS_MIXER + hq) * HEAD_DIM
                o_ref[row:row + HEAD_DIM, :] = o.astype(BF16)


def _ctx_attention(sink, qgt, kg, vgt, qlt, kl, vlt, dims):
    nb, s, n_ctx = dims["batch"], dims["seq"], dims["n_ctx"]
    ctx_blk0 = (nb * s) // n_ctx
    def q_spec(depth):
        return pl.BlockSpec((N_HEADS_MIXER, depth, n_ctx), lambda b: (0, 0, ctx_blk0 + b))

    def k_spec(depth):
        return pl.BlockSpec((KV_HEADS, n_ctx, depth), lambda b: (0, ctx_blk0 + b, 0))

    v_spec = pl.BlockSpec((KV_HEADS, HEAD_DIM, n_ctx), lambda b: (0, 0, ctx_blk0 + b))
    rows = 2 * N_HEADS_MIXER * HEAD_DIM
    return pl.pallas_call(
        _ctx_kernel,
        grid=(nb,),
        in_specs=[pl.BlockSpec(memory_space=pltpu.SMEM), q_spec(QK_DEPTH), k_spec(QK_DEPTH), v_spec,
                  q_spec(LANES), k_spec(LANES), v_spec],
        out_specs=pl.BlockSpec((rows, n_ctx), lambda b: (0, b)),
        out_shape=jax.ShapeDtypeStruct((rows, nb * n_ctx), BF16),
        compiler_params=pltpu.CompilerParams(dimension_semantics=("arbitrary",), vmem_limit_bytes=VMEM_LIMIT),
        name="ctx_attention",
    )(sink, qgt, kg, vgt, qlt, kl, vlt)


def _route(sel, scores):
    n = EXPERTS_PER_GROUP
    group_score = []
    for g in range(N_GROUPS):
        r = sel[g * n:(g + 1) * n]
        best = None
        for a in range(n):
            for b in range(a + 1, n):
                pair = r[a] + r[b]
                best = pair if best is None else jnp.maximum(best, pair)
        group_score.append(best)
    gates = [None] * N_EXPERTS
    picked = []
    for g in range(N_GROUPS):
        chosen = None
        for h in range(N_GROUPS):
            if h == g:
                continue
            c = (group_score[g] > group_score[h]) if h < g else (group_score[g] >= group_score[h])
            chosen = c if chosen is None else (chosen & c)
        for a in range(n):
            ea = g * n + a
            rank = jnp.zeros_like(sel[ea])
            for b in range(n):
                if b == a:
                    continue
                eb = g * n + b
                ahead = (sel[eb] >= sel[ea]) if b < a else (sel[eb] > sel[ea])
                rank = rank + jnp.where(ahead, 1.0, 0.0)
            picked.append(chosen & (rank < 2.0))
    den = None
    for e in range(N_EXPERTS):
        w = jnp.where(picked[e], scores[e], 0.0)
        gates[e] = w
        den = w if den is None else den + w
    cls = jnp.zeros_like(den)
    for g in range(N_GROUPS):
        for code, (a, b) in enumerate(EXPERT_PAIRS):
            both = picked[g * n + a] & picked[g * n + b]
            cls = cls + jnp.where(both, float(g * len(EXPERT_PAIRS) + code), 0.0)
    return [w / den for w in gates], cls


def _post_kernel(yg_ref, yl_ref, yc_ref, x_ref, mod_ref, wout_ref, g_ref, rwt_ref, rb_ref,
                 x1_ref, h2x_ref, meta_ref, *, n_lat):
    i = pl.program_id(0)
    d = x_ref.shape[1]
    tm = x_ref.shape[0]
    yt_lat = jnp.concatenate([yg_ref[...], yl_ref[...]], axis=0)
    is_lat = jnp.full(yt_lat.shape, i, jnp.int32) < n_lat
    yt = jnp.where(is_lat, yt_lat, yc_ref[...])
    a = lax.dot_general(yt, wout_ref[...], (((0,), (0,)), ((), ())), preferred_element_type=F32)
    gate1 = mod_ref[:, 2 * d:3 * d]
    shift2 = mod_ref[:, 3 * d:4 * d]
    scale2 = mod_ref[:, 4 * d:5 * d]
    x1 = x_ref[...] + gate1 * a
    x1_ref[...] = x1
    h2 = _rms_rows(x1, g_ref[...]) * (1.0 + scale2) + shift2
    logits_t = lax.dot_general(rwt_ref[...], h2.astype(BF16), (((1,), (1,)), ((), ())),
                               preferred_element_type=F32)
    scores_t = _sigmoid(logits_t)
    sel_t = scores_t + rb_ref[...]
    gates, cls = _route([sel_t[e:e + 1] for e in range(N_EXPERTS)], [scores_t[e:e + 1] for e in range(N_EXPERTS)])
    gates_t = jnp.concatenate(gates + [jnp.zeros((LANES - N_EXPERTS, tm), F32)], axis=0)
    h2x_ref[:, 0:d] = h2
    h2x_ref[:, d:d + LANES] = gates_t.T
    meta_ref[...] = jnp.concatenate([cls, jnp.zeros((7, tm), F32)], axis=0)


def _post_attention(ygt, ylt, yct, x_all, mod, l, w_out_b, ffn_g, rw_t, rb_col, dims):
    ntok, d = x_all.shape
    tm = TOKEN_TILE
    n_lat, per_b, nb = dims["n_lat_tiles"], dims["tiles_per_batch"], dims["batch"]
    rows = ygt.shape[0]

    def mod_idx(i):
        return (l, jnp.minimum(i // per_b, nb), 0, 0)

    def lat_idx(i):
        return (0, jnp.minimum(i, n_lat - 1))

    return pl.pallas_call(
        functools.partial(_post_kernel, n_lat=n_lat),
        grid=(ntok // tm,),
        in_specs=[pl.BlockSpec((rows, tm), lat_idx),
                  pl.BlockSpec((rows, tm), lat_idx),
                  pl.BlockSpec((2 * rows, tm), lambda i: (0, 0)),
                  pl.BlockSpec((tm, d), lambda i: (i, 0)),
                  pl.BlockSpec((None, None, 1, mod.shape[-1]), mod_idx),
                  pl.BlockSpec(w_out_b.shape, lambda i: (0, 0)),
                  pl.BlockSpec((1, d), lambda i: (0, 0)),
                  pl.BlockSpec(rw_t.shape, lambda i: (0, 0)),
                  pl.BlockSpec(rb_col.shape, lambda i: (0, 0))],
        out_specs=[pl.BlockSpec((tm, d), lambda i: (i, 0)),
                   pl.BlockSpec((tm, d + LANES), lambda i: (i, 0)),
                   pl.BlockSpec((8, tm), lambda i: (0, i))],
        out_shape=[jax.ShapeDtypeStruct((ntok, d), F32),
                   jax.ShapeDtypeStruct((ntok, d + LANES), F32),
                   jax.ShapeDtypeStruct((8, ntok), F32)],
        compiler_params=pltpu.CompilerParams(dimension_semantics=("arbitrary",), vmem_limit_bytes=VMEM_LIMIT),
        name="post_attention",
    )(ygt, ylt, yct, x_all, mod, w_out_b, ffn_g, rw_t, rb_col)


def _sort_plan(cls, tm):
    ntok = cls.shape[0]
    p_tiles = -(-ntok // tm) + N_CLASSES
    order = jnp.argsort(cls, stable=True).astype(jnp.int32)
    classes = jnp.arange(N_CLASSES, dtype=jnp.int32)
    counts = jnp.sum((cls[:, None] == classes[None, :]).astype(jnp.int32), axis=0)
    tiles_per = (counts + tm - 1) // tm
    tile_end = jnp.cumsum(tiles_per)
    start = jnp.cumsum(counts) - counts
    tile = jnp.arange(p_tiles, dtype=jnp.int32)
    tile_cls = jnp.minimum(jnp.sum((tile_end[None, :] <= tile[:, None]).astype(jnp.int32), axis=1), N_CLASSES - 1)
    is_cls = tile_cls[:, None] == classes[None, :]

    def per_tile(v):
        return jnp.sum(jnp.where(is_cls, v[None, :], 0), axis=1)

    row0 = (tile - per_tile(tile_end - tiles_per)) * tm
    left = jnp.where(tile < tile_end[N_CLASSES - 1], per_tile(counts) - row0, 0)
    first = jnp.clip(per_tile(start) + row0, 0, ntok)
    order_pad = jnp.concatenate([order, jnp.zeros((tm,), jnp.int32)])
    token = jax.vmap(lambda f: lax.dynamic_slice(order_pad, (f,), (tm,)))(first)
    r = jnp.arange(tm, dtype=jnp.int32)[None, :]
    valid = r < left[:, None]
    src = jnp.where(valid, token, 0).astype(jnp.int32)
    dst = jnp.where(valid, token, ntok + r).astype(jnp.int32)
    pair = jnp.asarray(EXPERT_PAIRS, jnp.int32)[tile_cls % len(EXPERT_PAIRS)]
    base = (tile_cls // len(EXPERT_PAIRS)) * EXPERTS_PER_GROUP
    return dict(src=src.reshape(p_tiles, 1, tm), dst=dst.reshape(p_tiles, 1, tm), ea=base + pair[:, 0],
                eb=base + pair[:, 1], n_used=tile_end[N_CLASSES - 1:].astype(jnp.int32), p_tiles=p_tiles)


ROW_UNROLL = 8


def _row_copy_all(idx_ref, n_rows, make_copy):
    def body(i, carry):
        for j in range(ROW_UNROLL):
            r = i * ROW_UNROLL + j
            make_copy(idx_ref[0, 0, r], r).start(priority=j % 2)
        return carry
    lax.fori_loop(0, n_rows // ROW_UNROLL, body, 0)


def _gather_kernel(src_ref, h_hbm, o_hbm, buf, row_sem, out_sem):
    t = pl.program_id(0)
    last = pl.num_programs(0) - 1
    n_rows = buf.shape[1]
    slot = t % 2
    other = 1 - slot

    def rows_done(s):
        pltpu.make_async_copy(h_hbm.at[pl.ds(0, n_rows)], buf.at[s], row_sem.at[s]).wait()

    def write_back(tile, s):
        return pltpu.make_async_copy(buf.at[s], o_hbm.at[pl.ds(tile * n_rows, n_rows)], out_sem.at[s])

    @pl.when(t >= 2)
    def _():
        write_back(t - 2, slot).wait()

    _row_copy_all(src_ref, n_rows, lambda tok, r: pltpu.make_async_copy(
        h_hbm.at[pl.ds(tok, 1)], buf.at[slot, pl.ds(r, 1)], row_sem.at[slot]))

    @pl.when(t >= 1)
    def _():
        rows_done(other)
        write_back(t - 1, other).start()

    @pl.when(t == last)
    def _():
        rows_done(slot)
        write_back(t, slot).start()

        @pl.when(t >= 1)
        def _():
            write_back(t - 1, other).wait()

        write_back(t, slot).wait()


def _gather_rows(src, h2x, tm):
    p_tiles = src.shape[0]
    width = h2x.shape[1]
    return pl.pallas_call(
        _gather_kernel,
        grid=(p_tiles,),
        in_specs=[pl.BlockSpec((1, 1, tm), lambda t: (t, 0, 0), memory_space=pltpu.SMEM),
                  pl.BlockSpec(memory_space=pl.ANY)],
        out_specs=pl.BlockSpec(memory_space=pl.ANY),
        out_shape=jax.ShapeDtypeStruct((p_tiles * tm, width), F32),
        scratch_shapes=[pltpu.VMEM((2, tm, width), F32), pltpu.SemaphoreType.DMA((2,)), pltpu.SemaphoreType.DMA((2,))],
        compiler_params=pltpu.CompilerParams(dimension_semantics=("arbitrary",), vmem_limit_bytes=VMEM_LIMIT),
        name="moe_gather",
    )(src, h2x)


def _scatter_kernel(dst_ref, y_hbm, o_hbm, buf, in_sem, row_sem):
    t = pl.program_id(0)
    last = pl.num_programs(0) - 1
    n_rows = buf.shape[1]
    slot = t % 2
    other = 1 - slot

    def load(tile, s):
        return pltpu.make_async_copy(y_hbm.at[pl.ds(tile * n_rows, n_rows)], buf.at[s], in_sem.at[s])

    def rows_done(s):
        pltpu.make_async_copy(buf.at[s], o_hbm.at[pl.ds(0, n_rows)], row_sem.at[s]).wait()

    @pl.when(t == 0)
    def _():
        load(0, 0).start()

    @pl.when(t >= 1)
    def _():
        rows_done(other)

    @pl.when(t < last)
    def _():
        load(t + 1, other).start()

    load(t, slot).wait()
    _row_copy_all(dst_ref, n_rows, lambda tok, r: pltpu.make_async_copy(
        buf.at[slot, pl.ds(r, 1)], o_hbm.at[pl.ds(tok, 1)], row_sem.at[slot]))

    @pl.when(t == last)
    def _():
        rows_done(slot)


def _scatter_rows(dst, ys, ntok, tm):
    p_tiles = dst.shape[0]
    d = ys.shape[1]
    return pl.pallas_call(
        _scatter_kernel,
        grid=(p_tiles,),
        in_specs=[pl.BlockSpec((1, 1, tm), lambda t: (t, 0, 0), memory_space=pltpu.SMEM),
                  pl.BlockSpec(memory_space=pl.ANY)],
        out_specs=pl.BlockSpec(memory_space=pl.ANY),
        out_shape=jax.ShapeDtypeStruct((ntok + tm, d), F32),
        scratch_shapes=[pltpu.VMEM((2, tm, d), F32), pltpu.SemaphoreType.DMA((2,)), pltpu.SemaphoreType.DMA((2,))],
        compiler_params=pltpu.CompilerParams(dimension_semantics=("arbitrary",), vmem_limit_bytes=VMEM_LIMIT),
        name="moe_scatter",
    )(dst, ys)


def _expert_pair_kernel(ea_ref, eb_ref, nu_ref, x_ref, wga_ref, wua_ref, wda_ref, wgb_ref, wub_ref, wdb_ref, o_ref):
    t = pl.program_id(0)
    d = o_ref.shape[1]

    @pl.when(t < nu_ref[0])
    def _():
        h = x_ref[:, 0:d].astype(BF16)
        gates = x_ref[:, d:d + LANES]
        lane = lax.broadcasted_iota(jnp.int32, gates.shape, 1)
        y = None
        for e_ref, wg_ref, wu_ref, wd_ref in ((ea_ref, wga_ref, wua_ref, wda_ref), (eb_ref, wgb_ref, wub_ref, wdb_ref)):
            gcol = jnp.sum(jnp.where(lane == e_ref[t], gates, 0.0), axis=1, keepdims=True)
            a = _dot(h, wg_ref[0, 0])
            u = _dot(h, wu_ref[0, 0])
            he = ((a * _sigmoid(a)) * u * gcol).astype(BF16)
            part = _dot(he, wd_ref[0, 0])
            y = part if y is None else y + part
        o_ref[...] = y


def _expert_pairs(plan, hs, wg_b, wu_b, wd_b, l, tm):
    d = wd_b.shape[-1]
    f = wg_b.shape[-1]
    width = hs.shape[1]

    def w_spec(shape, which):
        return pl.BlockSpec((1, 1) + shape, lambda t, ea, eb, nu: (l, (ea, eb)[which][t], 0, 0))

    return pl.pallas_call(
        _expert_pair_kernel,
        grid_spec=pltpu.PrefetchScalarGridSpec(
            num_scalar_prefetch=3,
            grid=(plan["p_tiles"],),
            in_specs=[pl.BlockSpec((tm, width), lambda t, ea, eb, nu: (t, 0)),
                      w_spec((d, f), 0), w_spec((d, f), 0), w_spec((f, d), 0),
                      w_spec((d, f), 1), w_spec((d, f), 1), w_spec((f, d), 1)],
            out_specs=pl.BlockSpec((tm, d), lambda t, ea, eb, nu: (t, 0))),
        out_shape=jax.ShapeDtypeStruct((plan["p_tiles"] * tm, d), F32),
        compiler_params=pltpu.CompilerParams(dimension_semantics=("arbitrary",), vmem_limit_bytes=VMEM_LIMIT),
        name="moe_expert_pairs",
    )(plan["ea"], plan["eb"], plan["n_used"], hs, wg_b, wu_b, wd_b, wg_b, wu_b, wd_b)


def _moe(h2x, meta, wg_b, wu_b, wd_b, l):
    ntok = h2x.shape[0]
    tm = MOE_TILE
    plan = _sort_plan(meta[0].astype(jnp.int32), tm)
    hs = _gather_rows(plan["src"], h2x, tm)
    ys = _expert_pairs(plan, hs, wg_b, wu_b, wd_b, l, tm)
    return _scatter_rows(plan["dst"], ys, ntok, tm)


def _combine_kernel(x1_ref, yf_ref, mod_ref, o_ref):
    d = x1_ref.shape[1]
    o_ref[...] = x1_ref[...] + mod_ref[:, 5 * d:6 * d] * yf_ref[...]


def _combine(x1, yf, mod, l, n_rows, dims):
    d = x1.shape[1]
    tm = TOKEN_TILE
    per_b, nb = dims["tiles_per_batch"], dims["batch"]
    return pl.pallas_call(
        _combine_kernel,
        grid=(n_rows // tm,),
        in_specs=[pl.BlockSpec((tm, d), lambda i: (i, 0)),
                  pl.BlockSpec((tm, d), lambda i: (i, 0)),
                  pl.BlockSpec((None, None, 1, mod.shape[-1]), lambda i: (l, jnp.minimum(i // per_b, nb), 0, 0))],
        out_specs=pl.BlockSpec((tm, d), lambda i: (i, 0)),
        out_shape=jax.ShapeDtypeStruct((n_rows, d), F32),
        compiler_params=pltpu.CompilerParams(dimension_semantics=("arbitrary",), vmem_limit_bytes=VMEM_LIMIT),
        name="moe_combine",
    )(x1, yf, mod)


def _rope_tables(s, pad):
    rows = s // GRID_W
    row = jnp.repeat(jnp.arange(rows, dtype=jnp.int32), GRID_W).astype(F32)
    col = jnp.tile(jnp.arange(GRID_W, dtype=jnp.int32), rows).astype(F32)
    inv = ROPE_THETA ** (-jnp.arange(N_FREQ, dtype=F32) / N_FREQ)
    ang_r, ang_c = row[:, None] * inv, col[:, None] * inv
    cr, sr, cc, sc = jnp.cos(ang_r), jnp.sin(ang_r), jnp.cos(ang_c), jnp.sin(ang_c)
    one, zero = jnp.ones((pad, N_FREQ), F32), jnp.zeros((pad, N_FREQ), F32)
    cs_t = jnp.concatenate([jnp.concatenate([cr, sr, cc, sc], axis=1),
                            jnp.concatenate([one, zero, one, zero], axis=1)], axis=0).T
    cpat = jnp.concatenate([cr, cr, cc, cc], axis=1)
    spat = jnp.concatenate([-sr, sr, -sc, sc], axis=1)
    ident = jnp.concatenate([jnp.ones((pad, 2 * HEAD_DIM), F32), jnp.zeros((pad, 2 * HEAD_DIM), F32)], axis=1)
    cs_nat = jnp.concatenate([jnp.concatenate([cpat, cpat, spat, spat], axis=1), ident], axis=0)
    return cs_nat, cs_t


def kernel(x, c, ctx, c_ctx, ada_w, ada_b, attn_norm_g, ffn_norm_g, w_in, qn_global, kn_global, qn_local, kn_local,
           sink_logit, w_out, router_w, router_b, w_gate, w_up, w_down):
    nb, s, d = x.shape
    n_ctx = ctx.shape[1]
    depth = ada_w.shape[0]
    tm = TOKEN_TILE
    assert s % tm == 0 and nb * n_ctx == tm and s % GRID_W == 0 and nb + 1 <= MOD_ROWS
    dims = dict(batch=nb, seq=s, n_ctx=n_ctx, tiles_per_batch=s // tm, n_lat_tiles=nb * s // tm)

    cvec = jnp.concatenate([c, c_ctx[None, :], jnp.zeros((MOD_ROWS - nb - 1, d), F32)], axis=0)
    mod = _modulation(cvec, ada_w, ada_b)[:, :nb + 1].reshape(depth, nb + 1, 1, 6 * d)

    cs_nat, cs_t = _rope_tables(s, tm)
    head_of_lane = np.arange(LANES) // HEAD_DIM
    block_ones = jnp.asarray(head_of_lane[:, None] == head_of_lane[None, :], BF16)
    w_in_b = w_in.astype(BF16)
    w_out_b = w_out.astype(BF16)
    wg_b, wu_b, wd_b = w_gate.astype(BF16), w_up.astype(BF16), w_down.astype(BF16)
    rw_t = router_w.T.astype(BF16)
    rb_col = router_b.reshape(N_EXPERTS, 1).astype(F32)

    stream = (jnp.concatenate([x.reshape(nb * s, d), ctx.reshape(nb * n_ctx, d)], axis=0),)
    tq_g, tk_g = min(512, s), min(2048, s)
    tq_l = 256
    for l in range(depth):
        gq_cols = jnp.stack([qn_global[l], qn_local[l]]).reshape(2, HEAD_DIM, 1)
        gk_rows = jnp.stack([jnp.tile(kn_global[l], KV_HEADS), jnp.tile(kn_local[l], KV_HEADS)]).reshape(2, 1, LANES)
        outs = _pre_attention(
            stream, mod, l, attn_norm_g[l].reshape(1, d), w_in_b[l], gq_cols, gk_rows, block_ones, cs_nat, cs_t, dims)
        qgt, kg, vgt, qlt, kl, vlt = outs[:6]
        x_all = outs[6] if len(outs) > 6 else stream[0]
        ygt = _global_attention(qgt, kg, vgt, dims, tq_g, tk_g)
        ylt = _local_attention(sink_logit[l], qlt, kl, vlt, dims, tq_l)
        yct = _ctx_attention(sink_logit[l], qgt, kg, vgt, qlt, kl, vlt, dims)
        x1, h2x, meta = _post_attention(ygt, ylt, yct, x_all, mod, l, w_out_b[l], ffn_norm_g[l].reshape(1, d),
                                        rw_t, rb_col, dims)
        stream = (x1, _moe(h2x, meta, wg_b, wu_b, wd_b, l))
    return _combine(stream[0], stream[1], mod, depth - 1, nb * s, dims).reshape(nb, s, d)
```

```python
import functools
import math

import numpy as np
import jax
import jax.numpy as jnp
from jax import lax
from jax.experimental import pallas as pl
from jax.experimental.pallas import tpu as pltpu

F32 = jnp.float32
BF16 = jnp.bfloat16
FP8 = jnp.float8_e4m3fn

HEAD_DIM = 64
GRID_W = 64
WINDOW = 128
N_HEADS_MIXER = 8
KV_HEADS = 2
GROUP = N_HEADS_MIXER // KV_HEADS
ROPE_THETA = 10000.0
N_EXPERTS = 16
N_GROUPS = 4
EXPERTS_PER_GROUP = N_EXPERTS // N_GROUPS
EXPERT_PAIRS = tuple((a, b) for a in range(EXPERTS_PER_GROUP) for b in range(a + 1, EXPERTS_PER_GROUP))
N_CLASSES = N_GROUPS * len(EXPERT_PAIRS)
NORM_EPS = 1e-6
ATTN_SCALE = 1.0 / math.sqrt(HEAD_DIM)
LOG2_E = 1.4426950408889634
Q_SCALES = (ATTN_SCALE * LOG2_E, ATTN_SCALE)
NEG_INF = -1e30
N_FREQ = HEAD_DIM // 4

LANES = 128
TOKEN_TILE = 512
TOKEN_ROWS = 8
MOE_TILE = 256
QK_DEPTH = 256
R_ROW = 3 * HEAD_DIM
R_TILE = 32
R_MAX = 256.0
P_SHIFT = 3.0
P_LIMIT = 256.0
ONES_ROWS = 16
V_ROWS = HEAD_DIM + ONES_ROWS
VMEM_LIMIT = 48 * 1024 * 1024
MOD_ROWS = 16


def _dot(a, b):
    return jnp.dot(a, b, preferred_element_type=F32)


def _sigmoid(x):
    return 1.0 / (1.0 + jnp.exp(-x))


def _mod_kernel(c_ref, w_ref, b_ref, o_ref):
    c = c_ref[...]
    s = (c * _sigmoid(c)).astype(BF16)
    o_ref[0] = _dot(s, w_ref[0].astype(BF16)) + b_ref[0]


def _modulation(cvec, ada_w, ada_b):
    depth, d, d6 = ada_w.shape
    col = 1536
    assert d6 % col == 0
    return pl.pallas_call(
        _mod_kernel,
        grid=(depth, d6 // col),
        in_specs=[pl.BlockSpec((MOD_ROWS, d), lambda l, j: (0, 0)),
                  pl.BlockSpec((1, d, col), lambda l, j: (l, 0, j)),
                  pl.BlockSpec((1, 1, col), lambda l, j: (l, 0, j))],
        out_specs=pl.BlockSpec((1, MOD_ROWS, col), lambda l, j: (l, 0, j)),
        out_shape=jax.ShapeDtypeStruct((depth, MOD_ROWS, d6), F32),
        compiler_params=pltpu.CompilerParams(dimension_semantics=("arbitrary", "arbitrary"),
                                             vmem_limit_bytes=VMEM_LIMIT),
        name="modulation",
    )(cvec, ada_w, ada_b.reshape(depth, 1, d6))


def _rms_rows(x, gain):
    ms = jnp.mean(x * x, axis=-1, keepdims=True)
    return x * lax.rsqrt(ms + NORM_EPS) * gain


def _rope_t(t, cs_t):
    n = N_FREQ
    cr, sr, cc, sc = cs_t[0:n], cs_t[n:2 * n], cs_t[2 * n:3 * n], cs_t[3 * n:4 * n]
    x1r, x2r, x1c, x2c = t[0:n], t[n:2 * n], t[2 * n:3 * n], t[3 * n:4 * n]
    return jnp.concatenate([x1r * cr - x2r * sr, x2r * cr + x1r * sr,
                            x1c * cc - x2c * sc, x2c * cc + x1c * sc], axis=0)


def _q_heads_t(y_pair, gain_col, cs_t, q_scale):
    t = y_pair.T
    outs = []
    for j in range(2):
        th = t[j * HEAD_DIM:(j + 1) * HEAD_DIM]
        ms = jnp.mean(th * th, axis=0, keepdims=True)
        thn = th * lax.rsqrt(ms + NORM_EPS) * gain_col
        outs.append(_rope_t(thn, cs_t) * q_scale)
    return outs


def _split_fp8(x):
    hi = x.astype(FP8).astype(F32)
    lo = (x - hi).astype(FP8).astype(F32)
    return hi, lo


def _k_heads(kc, block_ones, gain_row, cs):
    sq = kc * kc
    hi = sq.astype(BF16)
    lo = (sq - hi.astype(F32)).astype(BF16)
    ss = _dot(hi, block_ones) + _dot(lo, block_ones)
    kn = kc * lax.rsqrt(ss * (1.0 / HEAD_DIM) + NORM_EPS) * gain_row
    lane = lax.broadcasted_iota(jnp.int32, kn.shape, 1)
    first_half = (lane % (2 * N_FREQ)) < N_FREQ
    swapped = jnp.where(first_half, pltpu.roll(kn, LANES - N_FREQ, 1), pltpu.roll(kn, N_FREQ, 1))
    return kn * cs[:, :LANES] + swapped * cs[:, LANES:]


def _pre_kernel(*refs, combine):
    if combine:
        x1_ref, yf_ref, modp_ref = refs[:3]
        refs = refs[3:]
        xo_ref = refs[-1]
        refs = refs[:-1]
        tm, d = x1_ref.shape
        x = x1_ref[...] + modp_ref[:, 5 * d:6 * d] * _from_token_tiles(yf_ref, tm, d // LANES)
        xo_ref[...] = x
    else:
        x = refs[0][...]
        refs = refs[1:]
    (mod_ref, g_ref, w_ref, gq_ref, gk_ref, bones_ref, cs_ref, cst_ref,
     qgt_ref, kg_ref, vgt_ref, qlt_ref, kl_ref, vlt_ref) = refs
    tm, d = x.shape
    shift = mod_ref[:, 0:d]
    scale = mod_ref[:, d:2 * d]
    h = _rms_rows(x, g_ref[...]) * (1.0 + scale) + shift
    y = _dot(h.astype(BF16), w_ref[...])
    cs_t = cst_ref[...]
    cs = cs_ref[...]
    zeros = jnp.zeros((HEAD_DIM, tm), BF16)
    qw = N_HEADS_MIXER * HEAD_DIM
    kw = KV_HEADS * HEAD_DIM
    for mixer, (qt_ref, k_ref, vt_ref) in enumerate(((qgt_ref, kg_ref, vgt_ref), (qlt_ref, kl_ref, vlt_ref))):
        base = mixer * (qw + 2 * kw)
        gain_col = gq_ref[mixer]
        q_scale = Q_SCALES[mixer]
        split = mixer == 0
        for c in range(N_HEADS_MIXER // 2):
            pair = _q_heads_t(y[:, base + c * LANES: base + (c + 1) * LANES], gain_col, cs_t, q_scale)
            for j in range(2):
                if split:
                    hi, lo = _split_fp8(pair[j])
                    qt_ref[2 * c + j] = jnp.concatenate([hi, hi, lo, jnp.zeros_like(hi)], axis=0).astype(FP8)
                else:
                    qt_ref[2 * c + j] = jnp.concatenate([pair[j].astype(BF16), zeros], axis=0)
        kc = y[:, base + qw: base + qw + kw]
        kboth = _k_heads(kc, bones_ref[...], gk_ref[mixer], cs)
        lane = lax.broadcasted_iota(jnp.int32, kboth.shape, 1)
        low = lane < HEAD_DIM
        ones_col = jnp.where(lane == HEAD_DIM, 1.0, 0.0)
        if split:
            hi, lo = _split_fp8(kboth)
            hi_sw, lo_sw = pltpu.roll(hi, HEAD_DIM, 1), pltpu.roll(lo, HEAD_DIM, 1)
            k_ref[0] = jnp.concatenate([jnp.where(low, hi, lo_sw), jnp.where(low, hi, ones_col)], axis=1).astype(FP8)
            k_ref[1] = jnp.concatenate([jnp.where(low, hi_sw, lo), jnp.where(low, hi_sw, ones_col)], axis=1).astype(FP8)
        else:
            k_ref[0] = jnp.where(low, kboth, ones_col).astype(BF16)
            k_ref[1] = jnp.where(low, pltpu.roll(kboth, HEAD_DIM, 1), ones_col).astype(BF16)
        vc = y[:, base + qw + kw: base + qw + 2 * kw]
        vt = vc.T.astype(BF16)
        for kv in range(KV_HEADS):
            vt_ref[kv] = vt[kv * HEAD_DIM:(kv + 1) * HEAD_DIM]


def _pre_attention(stream, mod, l, attn_g, w_in_b, gq_cols, gk_rows, block_ones, cs_nat, cs_t, dims):
    combine = len(stream) == 2
    ntok, d = stream[0].shape
    tm = TOKEN_TILE
    n_lat, per_b, nb = dims["n_lat_tiles"], dims["tiles_per_batch"], dims["batch"]
    n_tiles = ntok // tm

    def mod_idx(i):
        return (l, jnp.minimum(i // per_b, nb), 0, 0)

    def tab_idx(i):
        return jnp.where(i < n_lat, i % per_b, per_b)

    qt_shape = jax.ShapeDtypeStruct((N_HEADS_MIXER, LANES, ntok), BF16)
    k_shape = jax.ShapeDtypeStruct((KV_HEADS, ntok, LANES), BF16)
    vt_shape = jax.ShapeDtypeStruct((KV_HEADS, HEAD_DIM, ntok), BF16)
    qt_spec = pl.BlockSpec((N_HEADS_MIXER, LANES, tm), lambda i: (0, 0, i))
    k_spec = pl.BlockSpec((KV_HEADS, tm, LANES), lambda i: (0, i, 0))
    vt_spec = pl.BlockSpec((KV_HEADS, HEAD_DIM, tm), lambda i: (0, 0, i))
    qt8_shape = jax.ShapeDtypeStruct((N_HEADS_MIXER, QK_DEPTH, ntok), FP8)
    k8_shape = jax.ShapeDtypeStruct((KV_HEADS, ntok, QK_DEPTH), FP8)
    qt8_spec = pl.BlockSpec((N_HEADS_MIXER, QK_DEPTH, tm), lambda i: (0, 0, i))
    k8_spec = pl.BlockSpec((KV_HEADS, tm, QK_DEPTH), lambda i: (0, i, 0))
    row_spec =pl.BlockSpec((tm, d), lambda i: (i, 0))
    mod_spec = pl.BlockSpec((None, None, 1, mod.shape[-1]), mod_idx)
    if combine:
        modp_spec = pl.BlockSpec((None, None, 1, mod.shape[-1]), lambda i: (l - 1,) + mod_idx(i)[1:])
        tile_spec = pl.BlockSpec((tm * TOKEN_ROWS, LANES), lambda i: (i, 0))
        stream_specs, stream_args = [row_spec, tile_spec, modp_spec], [stream[0], stream[1], mod]
        extra_specs, extra_shapes = [row_spec], [jax.ShapeDtypeStruct((ntok, d), F32)]
    else:
        stream_specs, stream_args, extra_specs, extra_shapes = [row_spec], [stream[0]], [], []
    return pl.pallas_call(
        functools.partial(_pre_kernel, combine=combine),
        grid=(n_tiles,),
        in_specs=stream_specs + [
                  mod_spec,
                  pl.BlockSpec((1, d), lambda i: (0, 0)),
                  pl.BlockSpec(w_in_b.shape, lambda i: (0, 0)),
                  pl.BlockSpec(gq_cols.shape, lambda i: (0, 0, 0)),
                  pl.BlockSpec(gk_rows.shape, lambda i: (0, 0, 0)),
                  pl.BlockSpec(block_ones.shape, lambda i: (0, 0)),
                  pl.BlockSpec((tm, 2 * LANES), lambda i: (tab_idx(i), 0)),
                  pl.BlockSpec((HEAD_DIM, tm), lambda i: (0, tab_idx(i)))],
        out_specs=[qt8_spec, k8_spec, vt_spec, qt_spec, k_spec, vt_spec] + extra_specs,
        out_shape=[qt8_shape, k8_shape, vt_shape, qt_shape, k_shape, vt_shape] + extra_shapes,
        compiler_params=pltpu.CompilerParams(dimension_semantics=("arbitrary",), vmem_limit_bytes=VMEM_LIMIT),
        name="pre_attention",
    )(*stream_args, mod, attn_g, w_in_b, gq_cols, gk_rows, block_ones, cs_nat, cs_t)


def _with_ones(vt):
    return jnp.concatenate([vt, jnp.ones((ONES_ROWS, vt.shape[1]), BF16)], axis=0)


def _round_ref(x):
    inside = jnp.abs(x) <= R_MAX
    return jnp.where(inside, jnp.clip(x, -R_MAX, R_MAX).astype(FP8).astype(F32), x)


def _flash_kernel(q_ref, k_ref, v_ref, kc_ref, vc_ref, o_ref, r_scr, acc_scr, new_scr, pmax_scr):
    ki = pl.program_id(3)
    tq = q_ref.shape[2]
    zpad = jnp.zeros((QK_DEPTH - R_ROW - R_TILE, tq), FP8)

    def q_ext(g, r_row):
        neg = jnp.concatenate([-r_row, jnp.zeros((R_TILE - 1, tq), F32)], axis=0).astype(FP8)
        return jnp.concatenate([q_ref[g, 0:R_ROW], neg, zpad], axis=0)

    def two_pass_update(g, k, v1, first):
        r_old = jnp.zeros((1, tq), F32) if first else r_scr[g]
        st = _dot(k, q_ref[g]) - r_old
        top = jnp.max(st, axis=0, keepdims=True) - P_SHIFT
        delta = _round_ref(r_old + (top if first else jnp.maximum(top, 0.0))) - r_old
        p = jnp.exp2(st - delta).astype(BF16)
        pv = _dot(v1, p)
        acc_scr[g] = pv if first else acc_scr[g] * jnp.exp2(-delta) + pv
        r_scr[g] = r_old + delta

    @pl.when(ki == 0)
    def _():
        vc1 = _with_ones(vc_ref[...])
        kc = kc_ref[...]
        for g in range(GROUP):
            two_pass_update(g, kc, vc1, True)

    v1 = _with_ones(v_ref[...])
    v8 = v1.astype(FP8)
    k = k_ref[...]
    for g in range(GROUP):
        st = _dot(k, q_ext(g, r_scr[g]))
        p = jnp.exp2(st.astype(BF16))
        pmax_scr[g] = jnp.max(p, axis=0, keepdims=True).astype(F32)
        new_scr[g] = acc_scr[g] + _dot(v8, p.astype(FP8))
    clean = (jnp.max(pmax_scr[...]) <= P_LIMIT) & (jnp.max(jnp.abs(r_scr[...])) <= R_MAX)

    @pl.when(clean)
    def _():
        for g in range(GROUP):
            r_old = r_scr[g]
            r_new = _round_ref(r_old + jnp.maximum(jnp.log2(pmax_scr[g]) - P_SHIFT, 0.0))
            acc_scr[g] = new_scr[g] * jnp.exp2(r_old - r_new)
            r_scr[g] = r_new

    @pl.when(jnp.logical_not(clean))
    def _():
        for g in range(GROUP):
            two_pass_update(g, k, v1, False)

    @pl.when(ki == pl.num_programs(3) - 1)
    def _():
        for g in range(GROUP):
            a = acc_scr[g]
            o_ref[g * HEAD_DIM:(g + 1) * HEAD_DIM, :] = (a[0:HEAD_DIM] / a[HEAD_DIM:HEAD_DIM + 1]).astype(BF16)


def _global_attention(qt, k, vt, dims, tq, tk):
    nb, s, n_ctx = dims["batch"], dims["seq"], dims["n_ctx"]
    nq, nk = s // tq, s // tk
    ctx_blk0 = (nb * s) // n_ctx
    return pl.pallas_call(
        _flash_kernel,
        grid=(nb, KV_HEADS, nq, nk),
        in_specs=[pl.BlockSpec((GROUP, QK_DEPTH, tq), lambda b, kv, qi, ki: (kv, 0, b * nq + qi)),
                  pl.BlockSpec((None, tk, QK_DEPTH), lambda b, kv, qi, ki: (kv, b * nk + ki, 0)),
                  pl.BlockSpec((None, HEAD_DIM, tk), lambda b, kv, qi, ki: (kv, 0, b * nk + ki)),
                  pl.BlockSpec((None, n_ctx, QK_DEPTH), lambda b, kv, qi, ki: (kv, ctx_blk0 + b, 0)),
                  pl.BlockSpec((None, HEAD_DIM, n_ctx), lambda b, kv, qi, ki: (kv, 0, ctx_blk0 + b))],
        out_specs=pl.BlockSpec((GROUP * HEAD_DIM, tq), lambda b, kv, qi, ki: (kv, b * nq + qi)),
        out_shape=jax.ShapeDtypeStruct((N_HEADS_MIXER * HEAD_DIM, nb * s), BF16),
        scratch_shapes=[pltpu.VMEM((GROUP, 1, tq), F32), pltpu.VMEM((GROUP, V_ROWS, tq), F32),
                        pltpu.VMEM((GROUP, V_ROWS, tq), F32), pltpu.VMEM((GROUP, 1, tq), F32)],
        compiler_params=pltpu.CompilerParams(
            dimension_semantics=("arbitrary", "arbitrary", "arbitrary", "arbitrary"), vmem_limit_bytes=VMEM_LIMIT),
        name="global_attention",
    )(qt, k, vt, k, vt)


def _single_pass_attend(k_all, v1_all, q_t, mask, sink, exp=jnp.exp):
    st = _dot(k_all, q_t)
    if mask is not None:
        st = jnp.where(mask, st, NEG_INF)
    m = jnp.max(st, axis=0, keepdims=True)
    if sink is not None:
        m = jnp.maximum(m, sink)
    p = exp(st - m).astype(BF16)
    a = _dot(v1_all, p)
    den = a[HEAD_DIM:HEAD_DIM + 1]
    if sink is not None:
        den = den + exp(sink - m)
    return a[0:HEAD_DIM] / den


def _local_kernel(sink_ref, q_ref, kp_ref, km_ref, kn_ref, kc_ref, vp_ref, vm_ref, vn_ref, vc_ref, o_ref):
    kv = pl.program_id(1)
    qi = pl.program_id(2)
    nq = pl.num_programs(2)
    tq = q_ref.shape[2]
    n_ctx = kc_ref.shape[0]
    span = tq + 2 * WINDOW
    k_all = jnp.concatenate([kp_ref[...], km_ref[...], kn_ref[...], kc_ref[...]], axis=0)
    v1_all = _with_ones(jnp.concatenate([vp_ref[...], vm_ref[...], vn_ref[...], vc_ref[...]], axis=1))
    shape = (span + n_ctx, tq)
    j = lax.broadcasted_iota(jnp.int32, shape, 0)
    i = lax.broadcasted_iota(jnp.int32, shape, 1)
    band = jnp.abs(j - WINDOW - i) <= WINDOW
    qi_v = jnp.full(shape, qi, jnp.int32)
    in_seq = ((j >= WINDOW) | (qi_v > 0)) & ((j < WINDOW + tq) | (qi_v < nq - 1))
    mask = (band & in_seq & (j < span)) | (j >= span)
    for g in range(GROUP):
        sink = jnp.full((1, tq), sink_ref[kv * GROUP + g], F32)
        o = _single_pass_attend(k_all, v1_all, q_ref[g], mask, sink)
        o_ref[g * HEAD_DIM:(g + 1) * HEAD_DIM, :] = o.astype(BF16)


def _local_attention(sink, qt, k, vt, dims, tq):
    nb, s, n_ctx = dims["batch"], dims["seq"], dims["n_ctx"]
    nq = s // tq
    w = WINDOW
    per_b_w = s // w
    r = tq // w
    ctx_blk0 = (nb * s) // n_ctx

    def prev_idx(b, qi):
        return b * per_b_w + jnp.maximum(qi * r - 1, 0)

    def next_idx(b, qi):
        return b * per_b_w + jnp.minimum(qi * r + r, per_b_w - 1)

    return pl.pallas_call(
        _local_kernel,
        grid=(nb, KV_HEADS, nq),
        in_specs=[pl.BlockSpec(memory_space=pltpu.SMEM),
                  pl.BlockSpec((GROUP, LANES, tq), lambda b, kv, qi: (kv, 0, b * nq + qi)),
                  pl.BlockSpec((None, w, LANES), lambda b, kv, qi: (kv, prev_idx(b, qi), 0)),
                  pl.BlockSpec((None, tq, LANES), lambda b, kv, qi: (kv, b * nq + qi, 0)),
                  pl.BlockSpec((None, w, LANES), lambda b, kv, qi: (kv, next_idx(b, qi), 0)),
                  pl.BlockSpec((None, n_ctx, LANES), lambda b, kv, qi: (kv, ctx_blk0 + b, 0)),
                  pl.BlockSpec((None, HEAD_DIM, w), lambda b, kv, qi: (kv, 0, prev_idx(b, qi))),
                  pl.BlockSpec((None, HEAD_DIM, tq), lambda b, kv, qi: (kv, 0, b * nq + qi)),
                  pl.BlockSpec((None, HEAD_DIM, w), lambda b, kv, qi: (kv, 0, next_idx(b, qi))),
                  pl.BlockSpec((None, HEAD_DIM, n_ctx), lambda b, kv, qi: (kv, 0, ctx_blk0 + b))],
        out_specs=pl.BlockSpec((GROUP * HEAD_DIM, tq), lambda b, kv, qi: (kv, b * nq + qi)),
        out_shape=jax.ShapeDtypeStruct((N_HEADS_MIXER * HEAD_DIM, nb * s), BF16),
        compiler_params=pltpu.CompilerParams(
            dimension_semantics=("arbitrary", "arbitrary", "arbitrary"), vmem_limit_bytes=VMEM_LIMIT),
        name="local_attention",
    )(sink, qt, k, k, k, k, vt, vt, vt, vt)


def _ctx_kernel(sink_ref, qg_ref, kg_ref, vg_ref, ql_ref, kl_ref, vl_ref, o_ref):
    tq = qg_ref.shape[2]
    for mixer, (q_ref, k_ref, v_ref) in enumerate(((qg_ref, kg_ref, vg_ref), (ql_ref, kl_ref, vl_ref))):
        for kv in range(KV_HEADS):
            k_all = k_ref[kv]
            v1 = _with_ones(v_ref[kv])
            for g in range(GROUP):
                hq = kv * GROUP + g
                sink = jnp.full((1, tq), sink_ref[hq], F32) if mixer == 1 else None
                o = _single_pass_attend(k_all, v1, q_ref[hq], None, sink, jnp.exp2 if mixer == 0 else jnp.exp)
                row = (mixer * N_HEADS_MIXER + hq) * HEAD_DIM
                o_ref[row:row + HEAD_DIM, :] = o.astype(BF16)


def _ctx_attention(sink, qgt, kg, vgt, qlt, kl, vlt, dims):
    nb, s, n_ctx = dims["batch"], dims["seq"], dims["n_ctx"]
    ctx_blk0 = (nb * s) // n_ctx
    def q_spec(depth):
        return pl.BlockSpec((N_HEADS_MIXER, depth, n_ctx), lambda b: (0, 0, ctx_blk0 + b))

    def k_spec(depth):
        return pl.BlockSpec((KV_HEADS, n_ctx, depth), lambda b: (0, ctx_blk0 + b, 0))

    v_spec = pl.BlockSpec((KV_HEADS, HEAD_DIM, n_ctx), lambda b: (0, 0, ctx_blk0 + b))
    rows = 2 * N_HEADS_MIXER * HEAD_DIM
    return pl.pallas_call(
        _ctx_kernel,
        grid=(nb,),
        in_specs=[pl.BlockSpec(memory_space=pltpu.SMEM), q_spec(QK_DEPTH), k_spec(QK_DEPTH), v_spec,
                  q_spec(LANES), k_spec(LANES), v_spec],
        out_specs=pl.BlockSpec((rows, n_ctx), lambda b: (0, b)),
        out_shape=jax.ShapeDtypeStruct((rows, nb * n_ctx), BF16),
        compiler_params=pltpu.CompilerParams(dimension_semantics=("arbitrary",), vmem_limit_bytes=VMEM_LIMIT),
        name="ctx_attention",
    )(sink, qgt, kg, vgt, qlt, kl, vlt)


def _to_token_tiles(ref, mat):
    t = mat.shape[0]
    for j in range(mat.shape[1] // LANES):
        ref[pl.ds(j, t, stride=TOKEN_ROWS), :] = mat[:, j * LANES:(j + 1) * LANES]


def _from_token_tiles(ref, t, n_chunks):
    return jnp.concatenate([ref[pl.ds(j, t, stride=TOKEN_ROWS), :] for j in range(n_chunks)], axis=1)


def _route(sel, scores):
    n = EXPERTS_PER_GROUP
    group_score = []
    for g in range(N_GROUPS):
        r = sel[g * n:(g + 1) * n]
        best = None
        for a in range(n):
            for b in range(a + 1, n):
                pair = r[a] + r[b]
                best = pair if best is None else jnp.maximum(best, pair)
        group_score.append(best)
    gates = [None] * N_EXPERTS
    picked = []
    for g in range(N_GROUPS):
        chosen = None
        for h in range(N_GROUPS):
            if h == g:
                continue
            c = (group_score[g] > group_score[h]) if h < g else (group_score[g] >= group_score[h])
            chosen = c if chosen is None else (chosen & c)
        for a in range(n):
            ea = g * n + a
            rank = jnp.zeros_like(sel[ea])
            for b in range(n):
                if b == a:
                    continue
                eb = g * n + b
                ahead = (sel[eb] >= sel[ea]) if b < a else (sel[eb] > sel[ea])
                rank = rank + jnp.where(ahead, 1.0, 0.0)
            picked.append(chosen & (rank < 2.0))
    den = None
    for e in range(N_EXPERTS):
        w = jnp.where(picked[e], scores[e], 0.0)
        gates[e] = w
        den = w if den is None else den + w
    cls = jnp.zeros_like(den)
    for g in range(N_GROUPS):
        for code, (a, b) in enumerate(EXPERT_PAIRS):
            both = picked[g * n + a] & picked[g * n + b]
            cls = cls + jnp.where(both, float(g * len(EXPERT_PAIRS) + code), 0.0)
    return [w / den for w in gates], cls


def _post_kernel(yg_ref, yl_ref, yc_ref, x_ref, mod_ref, wout_ref, g_ref, rwt_ref, rb_ref,
                 x1_ref, hp_ref, meta_ref, *, n_lat):
    i = pl.program_id(0)
    d = x_ref.shape[1]
    tm = x_ref.shape[0]
    yt_lat = jnp.concatenate([yg_ref[...], yl_ref[...]], axis=0)
    is_lat = jnp.full(yt_lat.shape, i, jnp.int32) < n_lat
    yt = jnp.where(is_lat, yt_lat, yc_ref[...])
    a = lax.dot_general(yt, wout_ref[...], (((0,), (0,)), ((), ())), preferred_element_type=F32)
    gate1 = mod_ref[:, 2 * d:3 * d]
    shift2 = mod_ref[:, 3 * d:4 * d]
    scale2 = mod_ref[:, 4 * d:5 * d]
    x1 = x_ref[...] + gate1 * a
    x1_ref[...] = x1
    h2 = _rms_rows(x1, g_ref[...]) * (1.0 + scale2) + shift2
    logits_t = lax.dot_general(rwt_ref[...], h2.astype(BF16), (((1,), (1,)), ((), ())),
                               preferred_element_type=F32)
    scores_t = _sigmoid(logits_t)
    sel_t = scores_t + rb_ref[...]
    gates, cls = _route([sel_t[e:e + 1] for e in range(N_EXPERTS)], [scores_t[e:e + 1] for e in range(N_EXPERTS)])
    gates_t = jnp.concatenate(gates + [jnp.zeros((LANES - N_EXPERTS, tm), F32)], axis=0)
    bits = lax.bitcast_convert_type(h2.astype(BF16).astype(F32), jnp.uint32)
    half = d // 2
    words = (bits[:, :half] & jnp.uint32(0xFFFF0000)) | (bits[:, half:] >> 16)
    spare = jnp.zeros((tm, TOKEN_ROWS * LANES - half - LANES), jnp.uint32)
    _to_token_tiles(hp_ref, jnp.concatenate([words, lax.bitcast_convert_type(gates_t.T, jnp.uint32), spare], axis=1))
    meta_ref[...] = jnp.concatenate([cls, jnp.zeros((7, tm), F32)], axis=0)


def _post_attention(ygt, ylt, yct, x_all, mod, l, w_out_b, ffn_g, rw_t, rb_col, dims):
    ntok, d = x_all.shape
    tm = TOKEN_TILE
    n_lat, per_b, nb = dims["n_lat_tiles"], dims["tiles_per_batch"], dims["batch"]
    rows = ygt.shape[0]

    def mod_idx(i):
        return (l, jnp.minimum(i // per_b, nb), 0, 0)

    def lat_idx(i):
        return (0, jnp.minimum(i, n_lat - 1))

    return pl.pallas_call(
        functools.partial(_post_kernel, n_lat=n_lat),
        grid=(ntok // tm,),
        in_specs=[pl.BlockSpec((rows, tm), lat_idx),
                  pl.BlockSpec((rows, tm), lat_idx),
                  pl.BlockSpec((2 * rows, tm), lambda i: (0, 0)),
                  pl.BlockSpec((tm, d), lambda i: (i, 0)),
                  pl.BlockSpec((None, None, 1, mod.shape[-1]), mod_idx),
                  pl.BlockSpec(w_out_b.shape, lambda i: (0, 0)),
                  pl.BlockSpec((1, d), lambda i: (0, 0)),
                  pl.BlockSpec(rw_t.shape, lambda i: (0, 0)),
                  pl.BlockSpec(rb_col.shape, lambda i: (0, 0))],
        out_specs=[pl.BlockSpec((tm, d), lambda i: (i, 0)),
                   pl.BlockSpec((tm * TOKEN_ROWS, LANES), lambda i: (i, 0)),
                   pl.BlockSpec((8, tm), lambda i: (0, i))],
        out_shape=[jax.ShapeDtypeStruct((ntok, d), F32),
                   jax.ShapeDtypeStruct((ntok * TOKEN_ROWS, LANES), jnp.uint32),
                   jax.ShapeDtypeStruct((8, ntok), F32)],
        compiler_params=pltpu.CompilerParams(dimension_semantics=("arbitrary",), vmem_limit_bytes=VMEM_LIMIT),
        name="post_attention",
    )(ygt, ylt, yct, x_all, mod, w_out_b, ffn_g, rw_t, rb_col)


def _sort_plan(cls, tm):
    ntok = cls.shape[0]
    p_tiles = -(-ntok // tm) + N_CLASSES
    order = jnp.argsort(cls, stable=True).astype(jnp.int32)
    classes = jnp.arange(N_CLASSES, dtype=jnp.int32)
    counts = jnp.sum((cls[:, None] == classes[None, :]).astype(jnp.int32), axis=0)
    tiles_per = (counts + tm - 1) // tm
    tile_end = jnp.cumsum(tiles_per)
    start = jnp.cumsum(counts) - counts
    tile = jnp.arange(p_tiles, dtype=jnp.int32)
    tile_cls = jnp.minimum(jnp.sum((tile_end[None, :] <= tile[:, None]).astype(jnp.int32), axis=1), N_CLASSES - 1)
    is_cls = tile_cls[:, None] == classes[None, :]

    def per_tile(v):
        return jnp.sum(jnp.where(is_cls, v[None, :], 0), axis=1)

    row0 = (tile - per_tile(tile_end - tiles_per)) * tm
    left = jnp.where(tile < tile_end[N_CLASSES - 1], per_tile(counts) - row0, 0)
    p = p_tiles * tm
    pad_before = (tile_end - tiles_per) * tm - start
    base = jnp.concatenate([jnp.zeros((p,), jnp.int32), order, jnp.zeros((p,), jnp.int32)])
    token = jnp.zeros((p_tiles, tm), jnp.int32)
    for c in range(N_CLASSES):
        shifted = lax.dynamic_slice(base, (p - pad_before[c],), (p,)).reshape(p_tiles, tm)
        token = jnp.where(is_cls[:, c:c + 1], shifted, token)
    r = jnp.arange(tm, dtype=jnp.int32)[None, :]
    valid = r < left[:, None]
    src = jnp.where(valid, token, 0).astype(jnp.int32)
    dst = jnp.where(valid, token, ntok + (tile % 2)[:, None] * tm + r).astype(jnp.int32)
    pair = jnp.asarray(EXPERT_PAIRS, jnp.int32)[tile_cls % len(EXPERT_PAIRS)]
    base_e = (tile_cls // len(EXPERT_PAIRS)) * EXPERTS_PER_GROUP
    return dict(src=src.reshape(p_tiles, 1, tm), dst=dst.reshape(p_tiles, 1, tm), ea=base_e + pair[:, 0],
                eb=base_e + pair[:, 1], n_used=tile_end[N_CLASSES - 1:].astype(jnp.int32), p_tiles=p_tiles)


ROW_UNROLL = 8


def _row_copy_all(idx_ref, n_rows, make_copy):
    def body(i, carry):
        for j in range(ROW_UNROLL):
            r = i * ROW_UNROLL + j
            make_copy(idx_ref[0, 0, r], r).start(priority=j % 2)
        return carry
    lax.fori_loop(0, n_rows // ROW_UNROLL, body, 0)


def _token_tile(ref, index):
    return ref.at[pl.ds(pl.multiple_of(index * TOKEN_ROWS, TOKEN_ROWS), TOKEN_ROWS)]


def _expert_pair_kernel(ea_ref, eb_ref, nu_ref, src_ref, nxt_ref, dst_ref, hp_hbm,
                        wga_ref, wua_ref, wda_ref, wgb_ref, wub_ref, wdb_ref, yf_hbm, hbuf, ybuf, gsem, ssem):
    t = pl.program_id(0)
    n_used = nu_ref[0]
    tm = src_ref.shape[2]
    d = wda_ref.shape[3]
    slot = t % 2
    other = 1 - slot

    def gather(idx_ref, s):
        _row_copy_all(idx_ref, tm, lambda tok, r: pltpu.make_async_copy(
            _token_tile(hp_hbm, tok), _token_tile(hbuf.at[s], r), gsem.at[s]))

    def gather_done(s):
        pltpu.make_async_copy(hp_hbm.at[pl.ds(0, tm * TOKEN_ROWS)], hbuf.at[s], gsem.at[s]).wait()

    def scatter_done(s):
        pltpu.make_async_copy(ybuf.at[s], yf_hbm.at[pl.ds(0, tm * TOKEN_ROWS)], ssem.at[s]).wait()

    @pl.when(t < n_used)
    def _():
        @pl.when(t == 0)
        def _():
            gather(src_ref, 0)

        @pl.when(t + 1 < n_used)
        def _():
            gather(nxt_ref, other)

        gather_done(slot)

        @pl.when(t >= 2)
        def _():
            scatter_done(slot)

        tiles = hbuf.at[slot]
        half_chunks = d // (2 * LANES)
        words = _from_token_tiles(tiles, tm, half_chunks)
        h = jnp.concatenate([lax.bitcast_convert_type(words & jnp.uint32(0xFFFF0000), F32).astype(BF16),
                             lax.bitcast_convert_type(words << 16, F32).astype(BF16)], axis=1)
        gates = lax.bitcast_convert_type(tiles[pl.ds(half_chunks, tm, stride=TOKEN_ROWS), :], F32)
        lane = lax.broadcasted_iota(jnp.int32, gates.shape, 1)
        y = None
        for e_ref, wg_ref, wu_ref, wd_ref in ((ea_ref, wga_ref, wua_ref, wda_ref), (eb_ref, wgb_ref, wub_ref, wdb_ref)):
            gcol = jnp.sum(jnp.where(lane == e_ref[t], gates, 0.0), axis=1, keepdims=True)
            a = _dot(h, wg_ref[0, 0])
            u = _dot(h, wu_ref[0, 0])
            he = ((a * _sigmoid(a)) * u * gcol).astype(BF16)
            part = _dot(he, wd_ref[0, 0])
            y = part if y is None else y + part
        _to_token_tiles(ybuf.at[slot], y)
        _row_copy_all(dst_ref, tm, lambda tok, r: pltpu.make_async_copy(
            _token_tile(ybuf.at[slot], r), _token_tile(yf_hbm, tok), ssem.at[slot]))

        @pl.when(t == n_used - 1)
        def _():
            @pl.when(t >= 1)
            def _():
                scatter_done(other)

            scatter_done(slot)


def _moe(hp, meta, wg_b, wu_b, wd_b, l):
    ntok = hp.shape[0] // TOKEN_ROWS
    tm = MOE_TILE
    d = wd_b.shape[-1]
    f = wg_b.shape[-1]
    plan = _sort_plan(meta[0].astype(jnp.int32), tm)
    p_tiles = plan["p_tiles"]

    def w_spec(shape, which):
        return pl.BlockSpec((1, 1) + shape, lambda t, ea, eb, nu: (l, (ea, eb)[which][t], 0, 0))

    def idx_spec(ahead):
        return pl.BlockSpec((1, 1, tm), lambda t, ea, eb, nu: (jnp.minimum(t + ahead, p_tiles - 1), 0, 0),
                            memory_space=pltpu.SMEM)

    return pl.pallas_call(
        _expert_pair_kernel,
        grid_spec=pltpu.PrefetchScalarGridSpec(
            num_scalar_prefetch=3,
            grid=(p_tiles,),
            in_specs=[idx_spec(0), idx_spec(1), idx_spec(0), pl.BlockSpec(memory_space=pl.ANY),
                      w_spec((d, f), 0), w_spec((d, f), 0), w_spec((f, d), 0),
                      w_spec((d, f), 1), w_spec((d, f), 1), w_spec((f, d), 1)],
            out_specs=pl.BlockSpec(memory_space=pl.ANY),
            scratch_shapes=[pltpu.VMEM((2, tm * TOKEN_ROWS, LANES), jnp.uint32),
                            pltpu.VMEM((2, tm * TOKEN_ROWS, LANES), F32),
                            pltpu.SemaphoreType.DMA((2,)), pltpu.SemaphoreType.DMA((2,))]),
        out_shape=jax.ShapeDtypeStruct(((ntok + 2 * tm) * TOKEN_ROWS, LANES), F32),
        compiler_params=pltpu.CompilerParams(dimension_semantics=("arbitrary",), vmem_limit_bytes=VMEM_LIMIT),
        name="moe_expert_pairs",
    )(plan["ea"], plan["eb"], plan["n_used"], plan["src"], plan["src"], plan["dst"], hp,
      wg_b, wu_b, wd_b, wg_b, wu_b, wd_b)


def _combine_kernel(x1_ref, yf_ref, mod_ref, o_ref):
    tm, d = x1_ref.shape
    o_ref[...] = x1_ref[...] + mod_ref[:, 5 * d:6 * d] * _from_token_tiles(yf_ref, tm, d // LANES)


def _combine(x1, yf, mod, l, n_rows, dims):
    d = x1.shape[1]
    tm = TOKEN_TILE
    per_b, nb = dims["tiles_per_batch"], dims["batch"]
    return pl.pallas_call(
        _combine_kernel,
        grid=(n_rows // tm,),
        in_specs=[pl.BlockSpec((tm, d), lambda i: (i, 0)),
                  pl.BlockSpec((tm * TOKEN_ROWS, LANES), lambda i: (i, 0)),
                  pl.BlockSpec((None, None, 1, mod.shape[-1]), lambda i: (l, jnp.minimum(i // per_b, nb), 0, 0))],
        out_specs=pl.BlockSpec((tm, d), lambda i: (i, 0)),
        out_shape=jax.ShapeDtypeStruct((n_rows, d), F32),
        compiler_params=pltpu.CompilerParams(dimension_semantics=("arbitrary",), vmem_limit_bytes=VMEM_LIMIT),
        name="moe_combine",
    )(x1, yf, mod)


def _rope_tables(s, pad):
    rows = s // GRID_W
    row = jnp.repeat(jnp.arange(rows, dtype=jnp.int32), GRID_W).astype(F32)
    col = jnp.tile(jnp.arange(GRID_W, dtype=jnp.int32), rows).astype(F32)
    inv = ROPE_THETA ** (-jnp.arange(N_FREQ, dtype=F32) / N_FREQ)
    ang_r, ang_c = row[:, None] * inv, col[:, None] * inv
    cr, sr, cc, sc = jnp.cos(ang_r), jnp.sin(ang_r), jnp.cos(ang_c), jnp.sin(ang_c)
    one, zero = jnp.ones((pad, N_FREQ), F32), jnp.zeros((pad, N_FREQ), F32)
    cs_t = jnp.concatenate([jnp.concatenate([cr, sr, cc, sc], axis=1),
                            jnp.concatenate([one, zero, one, zero], axis=1)], axis=0).T
    cpat = jnp.concatenate([cr, cr, cc, cc], axis=1)
    spat = jnp.concatenate([-sr, sr, -sc, sc], axis=1)
    ident = jnp.concatenate([jnp.ones((pad, 2 * HEAD_DIM), F32), jnp.zeros((pad, 2 * HEAD_DIM), F32)], axis=1)
    cs_nat = jnp.concatenate([jnp.concatenate([cpat, cpat, spat, spat], axis=1), ident], axis=0)
    return cs_nat, cs_t


def kernel(x, c, ctx, c_ctx, ada_w, ada_b, attn_norm_g, ffn_norm_g, w_in, qn_global, kn_global, qn_local, kn_local,
           sink_logit, w_out, router_w, router_b, w_gate, w_up, w_down):
    nb, s, d = x.shape
    n_ctx = ctx.shape[1]
    depth = ada_w.shape[0]
    tm = TOKEN_TILE
    assert s % tm == 0 and nb * n_ctx == tm and s % GRID_W == 0 and nb + 1 <= MOD_ROWS
    dims = dict(batch=nb, seq=s, n_ctx=n_ctx, tiles_per_batch=s // tm, n_lat_tiles=nb * s // tm)

    cvec = jnp.concatenate([c, c_ctx[None, :], jnp.zeros((MOD_ROWS - nb - 1, d), F32)], axis=0)
    mod = _modulation(cvec, ada_w, ada_b)[:, :nb + 1].reshape(depth, nb + 1, 1, 6 * d)

    cs_nat, cs_t = _rope_tables(s, tm)
    head_of_lane = np.arange(LANES) // HEAD_DIM
    block_ones = jnp.asarray(head_of_lane[:, None] == head_of_lane[None, :], BF16)
    w_in_b = w_in.astype(BF16)
    w_out_b = w_out.astype(BF16)
    wg_b, wu_b, wd_b = w_gate.astype(BF16), w_up.astype(BF16), w_down.astype(BF16)
    rw_t = router_w.T.astype(BF16)
    rb_col = router_b.reshape(N_EXPERTS, 1).astype(F32)

    stream = (jnp.concatenate([x.reshape(nb * s, d), ctx.reshape(nb * n_ctx, d)], axis=0),)
    tq_g, tk_g = min(512, s), min(2048, s)
    tq_l = 256
    for l in range(depth):
        gq_cols = jnp.stack([qn_global[l], qn_local[l]]).reshape(2, HEAD_DIM, 1)
        gk_rows = jnp.stack([jnp.tile(kn_global[l], KV_HEADS), jnp.tile(kn_local[l], KV_HEADS)]).reshape(2, 1, LANES)
        outs = _pre_attention(
            stream, mod, l, attn_norm_g[l].reshape(1, d), w_in_b[l], gq_cols, gk_rows, block_ones, cs_nat, cs_t, dims)
        qgt, kg, vgt, qlt, kl, vlt = outs[:6]
        x_all = outs[6] if len(outs) > 6 else stream[0]
        ygt = _global_attention(qgt, kg, vgt, dims, tq_g, tk_g)
        ylt = _local_attention(sink_logit[l], qlt, kl, vlt, dims, tq_l)
        yct = _ctx_attention(sink_logit[l], qgt, kg, vgt, qlt, kl, vlt, dims)
        x1, hp, meta = _post_attention(ygt, ylt, yct, x_all, mod, l, w_out_b[l], ffn_norm_g[l].reshape(1, d),
                                        rw_t, rb_col, dims)
        stream = (x1, _moe(hp, meta, wg_b, wu_b, wd_b, l))
    return _combine(stream[0], stream[1], mod, depth - 1, nb * s, dims).reshape(nb, s, d)
```

```python
import functools
import math

import numpy as np
import jax
import jax.numpy as jnp
from jax import lax
from jax.experimental import pallas as pl
from jax.experimental.pallas import tpu as pltpu

F32 = jnp.float32
BF16 = jnp.bfloat16
FP8 = jnp.float8_e4m3fn

HEAD_DIM = 64
GRID_W = 64
WINDOW = 128
N_HEADS_MIXER = 8
KV_HEADS = 2
GROUP = N_HEADS_MIXER // KV_HEADS
ROPE_THETA = 10000.0
N_EXPERTS = 16
N_GROUPS = 4
EXPERTS_PER_GROUP = N_EXPERTS // N_GROUPS
EXPERT_PAIRS = tuple((a, b) for a in range(EXPERTS_PER_GROUP) for b in range(a + 1, EXPERTS_PER_GROUP))
N_CLASSES = N_GROUPS * len(EXPERT_PAIRS)
NORM_EPS = 1e-6
ATTN_SCALE = 1.0 / math.sqrt(HEAD_DIM)
LOG2_E = 1.4426950408889634
Q_SCALE = ATTN_SCALE * LOG2_E
NEG_INF = -1e30
N_FREQ = HEAD_DIM // 4

LANES = 128
TOKEN_TILE = 512
TOKEN_ROWS = 8
MOE_TILE = 256
QK_DEPTH = 256
R_ROW = 3 * HEAD_DIM
R_TILE = 32
R_MAX = 256.0
P_SHIFT = 3.0
P_LIMIT = 256.0
ONES_ROWS = 16
V_ROWS = HEAD_DIM + ONES_ROWS
VMEM_LIMIT = 48 * 1024 * 1024
MOD_ROWS = 16


def _dot(a, b):
    return jnp.dot(a, b, preferred_element_type=F32)


def _sigmoid(x):
    return 1.0 / (1.0 + jnp.exp(-x))


def _mod_kernel(c_ref, w_ref, b_ref, o_ref):
    c = c_ref[...]
    s = (c * _sigmoid(c)).astype(BF16)
    o_ref[0] = _dot(s, w_ref[0].astype(BF16)) + b_ref[0]


def _modulation(cvec, ada_w, ada_b):
    depth, d, d6 = ada_w.shape
    col = 1536
    assert d6 % col == 0
    return pl.pallas_call(
        _mod_kernel,
        grid=(depth, d6 // col),
        in_specs=[pl.BlockSpec((MOD_ROWS, d), lambda l, j: (0, 0)),
                  pl.BlockSpec((1, d, col), lambda l, j: (l, 0, j)),
                  pl.BlockSpec((1, 1, col), lambda l, j: (l, 0, j))],
        out_specs=pl.BlockSpec((1, MOD_ROWS, col), lambda l, j: (l, 0, j)),
        out_shape=jax.ShapeDtypeStruct((depth, MOD_ROWS, d6), F32),
        compiler_params=pltpu.CompilerParams(dimension_semantics=("arbitrary", "arbitrary"),
                                             vmem_limit_bytes=VMEM_LIMIT),
        name="modulation",
    )(cvec, ada_w, ada_b.reshape(depth, 1, d6))


def _rms_rows(x, gain):
    ms = jnp.mean(x * x, axis=-1, keepdims=True)
    return x * lax.rsqrt(ms + NORM_EPS) * gain


def _rope_t(t, cs_t):
    n = N_FREQ
    cr, sr, cc, sc = cs_t[0:n], cs_t[n:2 * n], cs_t[2 * n:3 * n], cs_t[3 * n:4 * n]
    x1r, x2r, x1c, x2c = t[0:n], t[n:2 * n], t[2 * n:3 * n], t[3 * n:4 * n]
    return jnp.concatenate([x1r * cr - x2r * sr, x2r * cr + x1r * sr,
                            x1c * cc - x2c * sc, x2c * cc + x1c * sc], axis=0)


def _q_heads_t(y_pair, gain_col, cs_t, q_scale):
    t = y_pair.T
    outs = []
    for j in range(2):
        th = t[j * HEAD_DIM:(j + 1) * HEAD_DIM]
        ms = jnp.mean(th * th, axis=0, keepdims=True)
        thn = th * lax.rsqrt(ms + NORM_EPS) * gain_col
        outs.append(_rope_t(thn, cs_t) * q_scale)
    return outs


def _split_fp8(x):
    hi = x.astype(FP8).astype(F32)
    lo = (x - hi).astype(FP8).astype(F32)
    return hi, lo


def _k_heads(kc, block_ones, gain_row, cs):
    sq = kc * kc
    hi = sq.astype(BF16)
    lo = (sq - hi.astype(F32)).astype(BF16)
    ss = _dot(hi, block_ones) + _dot(lo, block_ones)
    kn = kc * lax.rsqrt(ss * (1.0 / HEAD_DIM) + NORM_EPS) * gain_row
    lane = lax.broadcasted_iota(jnp.int32, kn.shape, 1)
    first_half = (lane % (2 * N_FREQ)) < N_FREQ
    swapped = jnp.where(first_half, pltpu.roll(kn, LANES - N_FREQ, 1), pltpu.roll(kn, N_FREQ, 1))
    return kn * cs[:, :LANES] + swapped * cs[:, LANES:]


def _pre_kernel(*refs, combine):
    if combine:
        x1_ref, yf_ref, modp_ref = refs[:3]
        refs = refs[3:]
        xo_ref = refs[-1]
        refs = refs[:-1]
        tm, d = x1_ref.shape
        x = x1_ref[...] + modp_ref[:, 5 * d:6 * d] * _from_token_tiles(yf_ref, tm, d // LANES)
        xo_ref[...] = x
    else:
        x = refs[0][...]
        refs = refs[1:]
    (mod_ref, g_ref, w_ref, gq_ref, gk_ref, bones_ref, cs_ref, cst_ref,
     qgt_ref, kg_ref, vgt_ref, qlt_ref, kl_ref, vlt_ref) = refs
    tm, d = x.shape
    shift = mod_ref[:, 0:d]
    scale = mod_ref[:, d:2 * d]
    h = _rms_rows(x, g_ref[...]) * (1.0 + scale) + shift
    y = _dot(h.astype(BF16), w_ref[...])
    cs_t = cst_ref[...]
    cs = cs_ref[...]
    zeros = jnp.zeros((HEAD_DIM, tm), BF16)
    qw = N_HEADS_MIXER * HEAD_DIM
    kw = KV_HEADS * HEAD_DIM
    for mixer, (qt_ref, k_ref, vt_ref) in enumerate(((qgt_ref, kg_ref, vgt_ref), (qlt_ref, kl_ref, vlt_ref))):
        base = mixer * (qw + 2 * kw)
        gain_col = gq_ref[mixer]
        split = mixer == 0
        for c in range(N_HEADS_MIXER // 2):
            pair = _q_heads_t(y[:, base + c * LANES: base + (c + 1) * LANES], gain_col, cs_t, Q_SCALE)
            for j in range(2):
                if split:
                    hi, lo = _split_fp8(pair[j])
                    qt_ref[2 * c + j] = jnp.concatenate([hi, hi, lo, jnp.zeros_like(hi)], axis=0).astype(FP8)
                else:
                    qt_ref[2 * c + j] = jnp.concatenate([pair[j].astype(BF16), zeros], axis=0)
        kc = y[:, base + qw: base + qw + kw]
        kboth = _k_heads(kc, bones_ref[...], gk_ref[mixer], cs)
        lane = lax.broadcasted_iota(jnp.int32, kboth.shape, 1)
        low = lane < HEAD_DIM
        ones_col = jnp.where(lane == HEAD_DIM, 1.0, 0.0)
        if split:
            hi, lo = _split_fp8(kboth)
            hi_sw, lo_sw = pltpu.roll(hi, HEAD_DIM, 1), pltpu.roll(lo, HEAD_DIM, 1)
            k_ref[0] = jnp.concatenate([jnp.where(low, hi, lo_sw), jnp.where(low, hi, ones_col)], axis=1).astype(FP8)
            k_ref[1] = jnp.concatenate([jnp.where(low, hi_sw, lo), jnp.where(low, hi_sw, ones_col)], axis=1).astype(FP8)
        else:
            k_ref[0] = jnp.where(low, kboth, ones_col).astype(BF16)
            k_ref[1] = jnp.where(low, pltpu.roll(kboth, HEAD_DIM, 1), ones_col).astype(BF16)
        vc = y[:, base + qw + kw: base + qw + 2 * kw]
        vt = vc.T.astype(BF16)
        for kv in range(KV_HEADS):
            vt_ref[kv] = vt[kv * HEAD_DIM:(kv + 1) * HEAD_DIM]


def _pre_attention(stream, mod, l, attn_g, w_in_b, gq_cols, gk_rows, block_ones, cs_nat, cs_t, dims):
    combine = len(stream) == 2
    ntok, d = stream[0].shape
    tm = TOKEN_TILE
    n_lat, per_b, nb = dims["n_lat_tiles"], dims["tiles_per_batch"], dims["batch"]
    n_tiles = ntok // tm

    def mod_idx(i):
        return (l, jnp.minimum(i // per_b, nb), 0, 0)

    def tab_idx(i):
        return jnp.where(i < n_lat, i % per_b, per_b)

    qt_shape = jax.ShapeDtypeStruct((N_HEADS_MIXER, LANES, ntok), BF16)
    k_shape = jax.ShapeDtypeStruct((KV_HEADS, ntok, LANES), BF16)
    vt_shape = jax.ShapeDtypeStruct((KV_HEADS, HEAD_DIM, ntok), BF16)
    qt_spec = pl.BlockSpec((N_HEADS_MIXER, LANES, tm), lambda i: (0, 0, i))
    k_spec = pl.BlockSpec((KV_HEADS, tm, LANES), lambda i: (0, i, 0))
    vt_spec = pl.BlockSpec((KV_HEADS, HEAD_DIM, tm), lambda i: (0, 0, i))
    qt8_shape = jax.ShapeDtypeStruct((N_HEADS_MIXER, QK_DEPTH, ntok), FP8)
    k8_shape = jax.ShapeDtypeStruct((KV_HEADS, ntok, QK_DEPTH), FP8)
    qt8_spec = pl.BlockSpec((N_HEADS_MIXER, QK_DEPTH, tm), lambda i: (0, 0, i))
    k8_spec = pl.BlockSpec((KV_HEADS, tm, QK_DEPTH), lambda i: (0, i, 0))
    row_spec =pl.BlockSpec((tm, d), lambda i: (i, 0))
    mod_spec = pl.BlockSpec((None, None, 1, mod.shape[-1]), mod_idx)
    if combine:
        modp_spec = pl.BlockSpec((None, None, 1, mod.shape[-1]), lambda i: (l - 1,) + mod_idx(i)[1:])
        tile_spec = pl.BlockSpec((tm * TOKEN_ROWS, LANES), lambda i: (i, 0))
        stream_specs, stream_args = [row_spec, tile_spec, modp_spec], [stream[0], stream[1], mod]
        extra_specs, extra_shapes = [row_spec], [jax.ShapeDtypeStruct((ntok, d), F32)]
    else:
        stream_specs, stream_args, extra_specs, extra_shapes = [row_spec], [stream[0]], [], []
    return pl.pallas_call(
        functools.partial(_pre_kernel, combine=combine),
        grid=(n_tiles,),
        in_specs=stream_specs + [
                  mod_spec,
                  pl.BlockSpec((1, d), lambda i: (0, 0)),
                  pl.BlockSpec(w_in_b.shape, lambda i: (0, 0)),
                  pl.BlockSpec(gq_cols.shape, lambda i: (0, 0, 0)),
                  pl.BlockSpec(gk_rows.shape, lambda i: (0, 0, 0)),
                  pl.BlockSpec(block_ones.shape, lambda i: (0, 0)),
                  pl.BlockSpec((tm, 2 * LANES), lambda i: (tab_idx(i), 0)),
                  pl.BlockSpec((HEAD_DIM, tm), lambda i: (0, tab_idx(i)))],
        out_specs=[qt8_spec, k8_spec, vt_spec, qt_spec, k_spec, vt_spec] + extra_specs,
        out_shape=[qt8_shape, k8_shape, vt_shape, qt_shape, k_shape, vt_shape] + extra_shapes,
        compiler_params=pltpu.CompilerParams(dimension_semantics=("arbitrary",), vmem_limit_bytes=VMEM_LIMIT),
        name="pre_attention",
    )(*stream_args, mod, attn_g, w_in_b, gq_cols, gk_rows, block_ones, cs_nat, cs_t)


def _with_ones(vt):
    return jnp.concatenate([vt, jnp.ones((ONES_ROWS, vt.shape[1]), BF16)], axis=0)


def _round_ref(x):
    inside = jnp.abs(x) <= R_MAX
    return jnp.where(inside, jnp.clip(x, -R_MAX, R_MAX).astype(FP8).astype(F32), x)


def _flash_kernel(q_ref, k_ref, v_ref, kc_ref, vc_ref, o_ref, r_scr, acc_scr, new_scr, pmax_scr):
    ki = pl.program_id(3)
    tq = q_ref.shape[2]
    zpad = jnp.zeros((QK_DEPTH - R_ROW - R_TILE, tq), FP8)

    def q_ext(g, r_row):
        neg = jnp.concatenate([-r_row, jnp.zeros((R_TILE - 1, tq), F32)], axis=0).astype(FP8)
        return jnp.concatenate([q_ref[g, 0:R_ROW], neg, zpad], axis=0)

    def two_pass_update(g, k, v1, first):
        r_old = jnp.zeros((1, tq), F32) if first else r_scr[g]
        st = _dot(k, q_ref[g]) - r_old
        top = jnp.max(st, axis=0, keepdims=True) - P_SHIFT
        delta = _round_ref(r_old + (top if first else jnp.maximum(top, 0.0))) - r_old
        p = jnp.exp2(st - delta).astype(BF16)
        pv = _dot(v1, p)
        acc_scr[g] = pv if first else acc_scr[g] * jnp.exp2(-delta) + pv
        r_scr[g] = r_old + delta

    @pl.when(ki == 0)
    def _():
        vc1 = _with_ones(vc_ref[...])
        kc = kc_ref[...]
        for g in range(GROUP):
            two_pass_update(g, kc, vc1, True)

    v1 = _with_ones(v_ref[...])
    v8 = v1.astype(FP8)
    k = k_ref[...]
    for g in range(GROUP):
        st = _dot(k, q_ext(g, r_scr[g]))
        p = jnp.exp2(st.astype(BF16))
        pmax_scr[g] = jnp.max(p, axis=0, keepdims=True).astype(F32)
        new_scr[g] = acc_scr[g] + _dot(v8, p.astype(FP8))
    clean = (jnp.max(pmax_scr[...]) <= P_LIMIT) & (jnp.max(jnp.abs(r_scr[...])) <= R_MAX)

    @pl.when(clean)
    def _():
        for g in range(GROUP):
            r_old = r_scr[g]
            r_new = _round_ref(r_old + jnp.maximum(jnp.log2(pmax_scr[g]) - P_SHIFT, 0.0))
            acc_scr[g] = new_scr[g] * jnp.exp2(r_old - r_new)
            r_scr[g] = r_new

    @pl.when(jnp.logical_not(clean))
    def _():
        for g in range(GROUP):
            two_pass_update(g, k, v1, False)

    @pl.when(ki == pl.num_programs(3) - 1)
    def _():
        for g in range(GROUP):
            a = acc_scr[g]
            o_ref[g * HEAD_DIM:(g + 1) * HEAD_DIM, :] = (a[0:HEAD_DIM] / a[HEAD_DIM:HEAD_DIM + 1]).astype(BF16)


def _global_attention(qt, k, vt, dims, tq, tk):
    nb, s, n_ctx = dims["batch"], dims["seq"], dims["n_ctx"]
    nq, nk = s // tq, s // tk
    ctx_blk0 = (nb * s) // n_ctx
    return pl.pallas_call(
        _flash_kernel,
        grid=(nb, KV_HEADS, nq, nk),
        in_specs=[pl.BlockSpec((GROUP, QK_DEPTH, tq), lambda b, kv, qi, ki: (kv, 0, b * nq + qi)),
                  pl.BlockSpec((None, tk, QK_DEPTH), lambda b, kv, qi, ki: (kv, b * nk + ki, 0)),
                  pl.BlockSpec((None, HEAD_DIM, tk), lambda b, kv, qi, ki: (kv, 0, b * nk + ki)),
                  pl.BlockSpec((None, n_ctx, QK_DEPTH), lambda b, kv, qi, ki: (kv, ctx_blk0 + b, 0)),
                  pl.BlockSpec((None, HEAD_DIM, n_ctx), lambda b, kv, qi, ki: (kv, 0, ctx_blk0 + b))],
        out_specs=pl.BlockSpec((GROUP * HEAD_DIM, tq), lambda b, kv, qi, ki: (kv, b * nq + qi)),
        out_shape=jax.ShapeDtypeStruct((N_HEADS_MIXER * HEAD_DIM, nb * s), BF16),
        scratch_shapes=[pltpu.VMEM((GROUP, 1, tq), F32), pltpu.VMEM((GROUP, V_ROWS, tq), F32),
                        pltpu.VMEM((GROUP, V_ROWS, tq), F32), pltpu.VMEM((GROUP, 1, tq), F32)],
        compiler_params=pltpu.CompilerParams(
            dimension_semantics=("arbitrary", "arbitrary", "arbitrary", "arbitrary"), vmem_limit_bytes=VMEM_LIMIT),
        name="global_attention",
    )(qt, k, vt, k, vt)


def _single_pass_attend(k_all, v1_all, q_t, bias, sink):
    st = _dot(k_all, q_t)
    if bias is not None:
        st = st + bias
    m = jnp.max(st, axis=0, keepdims=True)
    if sink is not None:
        m = jnp.maximum(m, sink)
    p = jnp.exp2((st - m).astype(BF16))
    a = _dot(v1_all, p)
    den = a[HEAD_DIM:HEAD_DIM + 1]
    if sink is not None:
        den = den + jnp.exp2(sink - m)
    return a[0:HEAD_DIM] / den


def _band_bias(tq, n_ctx):
    span = tq + 2 * WINDOW
    j = np.arange(span + n_ctx)[:, None]
    i = np.arange(tq)[None, :]
    band = (np.abs(j - WINDOW - i) <= WINDOW) | (j >= span)
    out = []
    for case in range(4):
        first, last = case & 1, case & 2
        ok = band & ~((j < WINDOW) & bool(first)) & ~((j >= WINDOW + tq) & (j < span) & bool(last))
        out.append(np.where(ok, 0.0, NEG_INF))
    return jnp.asarray(np.stack(out), F32)


def _local_kernel(sink_ref, q_ref, kp_ref, km_ref, kn_ref, kc_ref, vp_ref, vm_ref, vn_ref, vc_ref, bias_ref, o_ref):
    kv = pl.program_id(1)
    qi = pl.program_id(2)
    tq = q_ref.shape[2]
    case = jnp.where(qi == 0, 1, 0) + jnp.where(qi == pl.num_programs(2) - 1, 2, 0)
    bias = bias_ref[case]
    k_all = jnp.concatenate([kp_ref[...], km_ref[...], kn_ref[...], kc_ref[...]], axis=0)
    v1_all = _with_ones(jnp.concatenate([vp_ref[...], vm_ref[...], vn_ref[...], vc_ref[...]], axis=1))
    q_all = jnp.concatenate([q_ref[g] for g in range(GROUP)], axis=1)
    sink = jnp.concatenate([jnp.full((1, tq), sink_ref[kv * GROUP + g] * LOG2_E, F32) for g in range(GROUP)], axis=1)
    o = _single_pass_attend(k_all, v1_all, q_all, jnp.concatenate([bias] * GROUP, axis=1), sink)
    for g in range(GROUP):
        o_ref[g * HEAD_DIM:(g + 1) * HEAD_DIM, :] = o[:, g * tq:(g + 1) * tq].astype(BF16)


def _local_attention(sink, qt, k, vt, dims, tq):
    nb, s, n_ctx = dims["batch"], dims["seq"], dims["n_ctx"]
    nq = s // tq
    w = WINDOW
    per_b_w = s // w
    r = tq // w
    ctx_blk0 = (nb * s) // n_ctx
    bias = _band_bias(tq, n_ctx)

    def prev_idx(b, qi):
        return b * per_b_w + jnp.maximum(qi * r - 1, 0)

    def next_idx(b, qi):
        return b * per_b_w + jnp.minimum(qi * r + r, per_b_w - 1)

    return pl.pallas_call(
        _local_kernel,
        grid=(nb, KV_HEADS, nq),
        in_specs=[pl.BlockSpec(memory_space=pltpu.SMEM),
                  pl.BlockSpec((GROUP, LANES, tq), lambda b, kv, qi: (kv, 0, b * nq + qi)),
                  pl.BlockSpec((None, w, LANES), lambda b, kv, qi: (kv, prev_idx(b, qi), 0)),
                  pl.BlockSpec((None, tq, LANES), lambda b, kv, qi: (kv, b * nq + qi, 0)),
                  pl.BlockSpec((None, w, LANES), lambda b, kv, qi: (kv, next_idx(b, qi), 0)),
                  pl.BlockSpec((None, n_ctx, LANES), lambda b, kv, qi: (kv, ctx_blk0 + b, 0)),
                  pl.BlockSpec((None, HEAD_DIM, w), lambda b, kv, qi: (kv, 0, prev_idx(b, qi))),
                  pl.BlockSpec((None, HEAD_DIM, tq), lambda b, kv, qi: (kv, 0, b * nq + qi)),
                  pl.BlockSpec((None, HEAD_DIM, w), lambda b, kv, qi: (kv, 0, next_idx(b, qi))),
                  pl.BlockSpec((None, HEAD_DIM, n_ctx), lambda b, kv, qi: (kv, 0, ctx_blk0 + b)),
                  pl.BlockSpec(bias.shape, lambda b, kv, qi: (0, 0, 0))],
        out_specs=pl.BlockSpec((GROUP * HEAD_DIM, tq), lambda b, kv, qi: (kv, b * nq + qi)),
        out_shape=jax.ShapeDtypeStruct((N_HEADS_MIXER * HEAD_DIM, nb * s), BF16),
        compiler_params=pltpu.CompilerParams(
            dimension_semantics=("arbitrary", "arbitrary", "arbitrary"), vmem_limit_bytes=VMEM_LIMIT),
        name="local_attention",
    )(sink, qt, k, k, k, k, vt, vt, vt, vt, bias)


def _ctx_kernel(sink_ref, qg_ref, kg_ref, vg_ref, ql_ref, kl_ref, vl_ref, o_ref):
    tq = qg_ref.shape[2]
    for mixer, (q_ref, k_ref, v_ref) in enumerate(((qg_ref, kg_ref, vg_ref), (ql_ref, kl_ref, vl_ref))):
        for kv in range(KV_HEADS):
            k_all = k_ref[kv]
            v1 = _with_ones(v_ref[kv])
            for g in range(GROUP):
                hq = kv * GROUP + g
                sink = jnp.full((1, tq), sink_ref[hq] * LOG2_E, F32) if mixer == 1 else None
                o = _single_pass_attend(k_all, v1, q_ref[hq], None, sink)
                row = (mixer * N_HEADS_MIXER + hq) * HEAD_DIM
                o_ref[row:row + HEAD_DIM, :] = o.astype(BF16)


def _ctx_attention(sink, qgt, kg, vgt, qlt, kl, vlt, dims):
    nb, s, n_ctx = dims["batch"], dims["seq"], dims["n_ctx"]
    ctx_blk0 = (nb * s) // n_ctx
    def q_spec(depth):
        return pl.BlockSpec((N_HEADS_MIXER, depth, n_ctx), lambda b: (0, 0, ctx_blk0 + b))

    def k_spec(depth):
        return pl.BlockSpec((KV_HEADS, n_ctx, depth), lambda b: (0, ctx_blk0 + b, 0))

    v_spec = pl.BlockSpec((KV_HEADS, HEAD_DIM, n_ctx), lambda b: (0, 0, ctx_blk0 + b))
    rows = 2 * N_HEADS_MIXER * HEAD_DIM
    return pl.pallas_call(
        _ctx_kernel,
        grid=(nb,),
        in_specs=[pl.BlockSpec(memory_space=pltpu.SMEM), q_spec(QK_DEPTH), k_spec(QK_DEPTH), v_spec,
                  q_spec(LANES), k_spec(LANES), v_spec],
        out_specs=pl.BlockSpec((rows, n_ctx), lambda b: (0, b)),
        out_shape=jax.ShapeDtypeStruct((rows, nb * n_ctx), BF16),
        compiler_params=pltpu.CompilerParams(dimension_semantics=("arbitrary",), vmem_limit_bytes=VMEM_LIMIT),
        name="ctx_attention",
    )(sink, qgt, kg, vgt, qlt, kl, vlt)


def _to_token_tiles(ref, mat):
    t = mat.shape[0]
    for j in range(mat.shape[1] // LANES):
        ref[pl.ds(j, t, stride=TOKEN_ROWS), :] = mat[:, j * LANES:(j + 1) * LANES]


def _from_token_tiles(ref, t, n_chunks):
    return jnp.concatenate([ref[pl.ds(j, t, stride=TOKEN_ROWS), :] for j in range(n_chunks)], axis=1)


def _route(sel, scores):
    n = EXPERTS_PER_GROUP
    group_score = []
    for g in range(N_GROUPS):
        r = sel[g * n:(g + 1) * n]
        best = None
        for a in range(n):
            for b in range(a + 1, n):
                pair = r[a] + r[b]
                best = pair if best is None else jnp.maximum(best, pair)
        group_score.append(best)
    gates = [None] * N_EXPERTS
    picked = []
    for g in range(N_GROUPS):
        chosen = None
        for h in range(N_GROUPS):
            if h == g:
                continue
            c = (group_score[g] > group_score[h]) if h < g else (group_score[g] >= group_score[h])
            chosen = c if chosen is None else (chosen & c)
        for a in range(n):
            ea = g * n + a
            rank = jnp.zeros_like(sel[ea])
            for b in range(n):
                if b == a:
                    continue
                eb = g * n + b
                ahead = (sel[eb] >= sel[ea]) if b < a else (sel[eb] > sel[ea])
                rank = rank + jnp.where(ahead, 1.0, 0.0)
            picked.append(chosen & (rank < 2.0))
    den = None
    for e in range(N_EXPERTS):
        w = jnp.where(picked[e], scores[e], 0.0)
        gates[e] = w
        den = w if den is None else den + w
    cls = jnp.zeros_like(den)
    for g in range(N_GROUPS):
        for code, (a, b) in enumerate(EXPERT_PAIRS):
            both = picked[g * n + a] & picked[g * n + b]
            cls = cls + jnp.where(both, float(g * len(EXPERT_PAIRS) + code), 0.0)
    return [w / den for w in gates], cls


def _post_kernel(yg_ref, yl_ref, yc_ref, x_ref, mod_ref, wout_ref, g_ref, rwt_ref, rb_ref,
                 x1_ref, hp_ref, meta_ref, *, n_lat):
    i = pl.program_id(0)
    d = x_ref.shape[1]
    tm = x_ref.shape[0]
    yt_lat = jnp.concatenate([yg_ref[...], yl_ref[...]], axis=0)
    is_lat = jnp.full(yt_lat.shape, i, jnp.int32) < n_lat
    yt = jnp.where(is_lat, yt_lat, yc_ref[...])
    a = lax.dot_general(yt, wout_ref[...], (((0,), (0,)), ((), ())), preferred_element_type=F32)
    gate1 = mod_ref[:, 2 * d:3 * d]
    shift2 = mod_ref[:, 3 * d:4 * d]
    scale2 = mod_ref[:, 4 * d:5 * d]
    x1 = x_ref[...] + gate1 * a
    x1_ref[...] = x1
    h2 = _rms_rows(x1, g_ref[...]) * (1.0 + scale2) + shift2
    logits_t = lax.dot_general(rwt_ref[...], h2.astype(BF16), (((1,), (1,)), ((), ())),
                               preferred_element_type=F32)
    scores_t = _sigmoid(logits_t)
    sel_t = scores_t + rb_ref[...]
    gates, cls = _route([sel_t[e:e + 1] for e in range(N_EXPERTS)], [scores_t[e:e + 1] for e in range(N_EXPERTS)])
    gates_t = jnp.concatenate(gates + [jnp.zeros((LANES - N_EXPERTS, tm), F32)], axis=0)
    bits = lax.bitcast_convert_type(h2.astype(BF16).astype(F32), jnp.uint32)
    half = d // 2
    words = (bits[:, :half] & jnp.uint32(0xFFFF0000)) | (bits[:, half:] >> 16)
    spare = jnp.zeros((tm, TOKEN_ROWS * LANES - half - LANES), jnp.uint32)
    _to_token_tiles(hp_ref, jnp.concatenate([words, lax.bitcast_convert_type(gates_t.T, jnp.uint32), spare], axis=1))
    meta_ref[...] = jnp.concatenate([cls, jnp.zeros((7, tm), F32)], axis=0)


def _post_attention(ygt, ylt, yct, x_all, mod, l, w_out_b, ffn_g, rw_t, rb_col, dims):
    ntok, d = x_all.shape
    tm = TOKEN_TILE
    n_lat, per_b, nb = dims["n_lat_tiles"], dims["tiles_per_batch"], dims["batch"]
    rows = ygt.shape[0]

    def mod_idx(i):
        return (l, jnp.minimum(i // per_b, nb), 0, 0)

    def lat_idx(i):
        return (0, jnp.minimum(i, n_lat - 1))

    return pl.pallas_call(
        functools.partial(_post_kernel, n_lat=n_lat),
        grid=(ntok // tm,),
        in_specs=[pl.BlockSpec((rows, tm), lat_idx),
                  pl.BlockSpec((rows, tm), lat_idx),
                  pl.BlockSpec((2 * rows, tm), lambda i: (0, 0)),
                  pl.BlockSpec((tm, d), lambda i: (i, 0)),
                  pl.BlockSpec((None, None, 1, mod.shape[-1]), mod_idx),
                  pl.BlockSpec(w_out_b.shape, lambda i: (0, 0)),
                  pl.BlockSpec((1, d), lambda i: (0, 0)),
                  pl.BlockSpec(rw_t.shape, lambda i: (0, 0)),
                  pl.BlockSpec(rb_col.shape, lambda i: (0, 0))],
        out_specs=[pl.BlockSpec((tm, d), lambda i: (i, 0)),
                   pl.BlockSpec((tm * TOKEN_ROWS, LANES), lambda i: (i, 0)),
                   pl.BlockSpec((8, tm), lambda i: (0, i))],
        out_shape=[jax.ShapeDtypeStruct((ntok, d), F32),
                   jax.ShapeDtypeStruct((ntok * TOKEN_ROWS, LANES), jnp.uint32),
                   jax.ShapeDtypeStruct((8, ntok), F32)],
        compiler_params=pltpu.CompilerParams(dimension_semantics=("arbitrary",), vmem_limit_bytes=VMEM_LIMIT),
        name="post_attention",
    )(ygt, ylt, yct, x_all, mod, w_out_b, ffn_g, rw_t, rb_col)


def _sort_plan(cls, tm):
    ntok = cls.shape[0]
    p_tiles = -(-ntok // tm) + N_CLASSES
    order = jnp.argsort(cls, stable=True).astype(jnp.int32)
    classes = jnp.arange(N_CLASSES, dtype=jnp.int32)
    counts = jnp.sum((cls[:, None] == classes[None, :]).astype(jnp.int32), axis=0)
    tiles_per = (counts + tm - 1) // tm
    tile_end = jnp.cumsum(tiles_per)
    start = jnp.cumsum(counts) - counts
    tile = jnp.arange(p_tiles, dtype=jnp.int32)
    tile_cls = jnp.minimum(jnp.sum((tile_end[None, :] <= tile[:, None]).astype(jnp.int32), axis=1), N_CLASSES - 1)
    is_cls = tile_cls[:, None] == classes[None, :]

    def per_tile(v):
        return jnp.sum(jnp.where(is_cls, v[None, :], 0), axis=1)

    row0 = (tile - per_tile(tile_end - tiles_per)) * tm
    left = jnp.where(tile < tile_end[N_CLASSES - 1], per_tile(counts) - row0, 0)
    p = p_tiles * tm
    pad_before = (tile_end - tiles_per) * tm - start
    base = jnp.concatenate([jnp.zeros((p,), jnp.int32), order, jnp.zeros((p,), jnp.int32)])
    token = jnp.zeros((p_tiles, tm), jnp.int32)
    for c in range(N_CLASSES):
        shifted = lax.dynamic_slice(base, (p - pad_before[c],), (p,)).reshape(p_tiles, tm)
        token = jnp.where(is_cls[:, c:c + 1], shifted, token)
    r = jnp.arange(tm, dtype=jnp.int32)[None, :]
    valid = r < left[:, None]
    src = jnp.where(valid, token, 0).astype(jnp.int32)
    dst = jnp.where(valid, token, ntok + (tile % 2)[:, None] * tm + r).astype(jnp.int32)
    pair = jnp.asarray(EXPERT_PAIRS, jnp.int32)[tile_cls % len(EXPERT_PAIRS)]
    base_e = (tile_cls // len(EXPERT_PAIRS)) * EXPERTS_PER_GROUP
    return dict(src=src.reshape(p_tiles, 1, tm), dst=dst.reshape(p_tiles, 1, tm), ea=base_e + pair[:, 0],
                eb=base_e + pair[:, 1], n_used=tile_end[N_CLASSES - 1:].astype(jnp.int32), p_tiles=p_tiles)


ROW_UNROLL = 8


def _row_copy_all(idx_ref, n_rows, make_copy):
    def body(i, carry):
        for j in range(ROW_UNROLL):
            r = i * ROW_UNROLL + j
            make_copy(idx_ref[0, 0, r], r).start(priority=j % 2)
        return carry
    lax.fori_loop(0, n_rows // ROW_UNROLL, body, 0)


def _token_tile(ref, index):
    return ref.at[pl.ds(pl.multiple_of(index * TOKEN_ROWS, TOKEN_ROWS), TOKEN_ROWS)]


def _expert_pair_kernel(ea_ref, eb_ref, nu_ref, src_ref, nxt_ref, dst_ref, hp_hbm,
                        wga_ref, wua_ref, wda_ref, wgb_ref, wub_ref, wdb_ref, yf_hbm, hbuf, ybuf, gsem, ssem):
    t = pl.program_id(0)
    n_used = nu_ref[0]
    tm = src_ref.shape[2]
    d = wda_ref.shape[3]
    slot = t % 2
    other = 1 - slot

    def gather(idx_ref, s):
        _row_copy_all(idx_ref, tm, lambda tok, r: pltpu.make_async_copy(
            _token_tile(hp_hbm, tok), _token_tile(hbuf.at[s], r), gsem.at[s]))

    def gather_done(s):
        pltpu.make_async_copy(hp_hbm.at[pl.ds(0, tm * TOKEN_ROWS)], hbuf.at[s], gsem.at[s]).wait()

    def scatter_done(s):
        pltpu.make_async_copy(ybuf.at[s], yf_hbm.at[pl.ds(0, tm * TOKEN_ROWS)], ssem.at[s]).wait()

    @pl.when(t < n_used)
    def _():
        @pl.when(t == 0)
        def _():
            gather(src_ref, 0)

        @pl.when(t + 1 < n_used)
        def _():
            gather(nxt_ref, other)

        gather_done(slot)

        @pl.when(t >= 2)
        def _():
            scatter_done(slot)

        tiles = hbuf.at[slot]
        half_chunks = d // (2 * LANES)
        words = _from_token_tiles(tiles, tm, half_chunks)
        h = jnp.concatenate([lax.bitcast_convert_type(words & jnp.uint32(0xFFFF0000), F32).astype(BF16),
                             lax.bitcast_convert_type(words << 16, F32).astype(BF16)], axis=1)
        gates = lax.bitcast_convert_type(tiles[pl.ds(half_chunks, tm, stride=TOKEN_ROWS), :], F32)
        lane = lax.broadcasted_iota(jnp.int32, gates.shape, 1)
        y = None
        for e_ref, wg_ref, wu_ref, wd_ref in ((ea_ref, wga_ref, wua_ref, wda_ref), (eb_ref, wgb_ref, wub_ref, wdb_ref)):
            gcol = jnp.sum(jnp.where(lane == e_ref[t], gates, 0.0), axis=1, keepdims=True)
            a = _dot(h, wg_ref[0, 0])
            u = _dot(h, wu_ref[0, 0])
            he = ((a * _sigmoid(a)) * u * gcol).astype(BF16)
            part = _dot(he, wd_ref[0, 0])
            y = part if y is None else y + part
        _to_token_tiles(ybuf.at[slot], y)
        _row_copy_all(dst_ref, tm, lambda tok, r: pltpu.make_async_copy(
            _token_tile(ybuf.at[slot], r), _token_tile(yf_hbm, tok), ssem.at[slot]))

        @pl.when(t == n_used - 1)
        def _():
            @pl.when(t >= 1)
            def _():
                scatter_done(other)

            scatter_done(slot)


def _moe(hp, meta, wg_b, wu_b, wd_b, l):
    ntok = hp.shape[0] // TOKEN_ROWS
    tm = MOE_TILE
    d = wd_b.shape[-1]
    f = wg_b.shape[-1]
    plan = _sort_plan(meta[0].astype(jnp.int32), tm)
    p_tiles = plan["p_tiles"]

    def w_spec(shape, which):
        return pl.BlockSpec((1, 1) + shape, lambda t, ea, eb, nu: (l, (ea, eb)[which][t], 0, 0))

    def idx_spec(ahead):
        return pl.BlockSpec((1, 1, tm), lambda t, ea, eb, nu: (jnp.minimum(t + ahead, p_tiles - 1), 0, 0),
                            memory_space=pltpu.SMEM)

    return pl.pallas_call(
        _expert_pair_kernel,
        grid_spec=pltpu.PrefetchScalarGridSpec(
            num_scalar_prefetch=3,
            grid=(p_tiles,),
            in_specs=[idx_spec(0), idx_spec(1), idx_spec(0), pl.BlockSpec(memory_space=pl.ANY),
                      w_spec((d, f), 0), w_spec((d, f), 0), w_spec((f, d), 0),
                      w_spec((d, f), 1), w_spec((d, f), 1), w_spec((f, d), 1)],
            out_specs=pl.BlockSpec(memory_space=pl.ANY),
            scratch_shapes=[pltpu.VMEM((2, tm * TOKEN_ROWS, LANES), jnp.uint32),
                            pltpu.VMEM((2, tm * TOKEN_ROWS, LANES), F32),
                            pltpu.SemaphoreType.DMA((2,)), pltpu.SemaphoreType.DMA((2,))]),
        out_shape=jax.ShapeDtypeStruct(((ntok + 2 * tm) * TOKEN_ROWS, LANES), F32),
        compiler_params=pltpu.CompilerParams(dimension_semantics=("arbitrary",), vmem_limit_bytes=VMEM_LIMIT),
        name="moe_expert_pairs",
    )(plan["ea"], plan["eb"], plan["n_used"], plan["src"], plan["src"], plan["dst"], hp,
      wg_b, wu_b, wd_b, wg_b, wu_b, wd_b)


def _combine_kernel(x1_ref, yf_ref, mod_ref, o_ref):
    tm, d = x1_ref.shape
    o_ref[...] = x1_ref[...] + mod_ref[:, 5 * d:6 * d] * _from_token_tiles(yf_ref, tm, d // LANES)


def _combine(x1, yf, mod, l, n_rows, dims):
    d = x1.shape[1]
    tm = TOKEN_TILE
    per_b, nb = dims["tiles_per_batch"], dims["batch"]
    return pl.pallas_call(
        _combine_kernel,
        grid=(n_rows // tm,),
        in_specs=[pl.BlockSpec((tm, d), lambda i: (i, 0)),
                  pl.BlockSpec((tm * TOKEN_ROWS, LANES), lambda i: (i, 0)),
                  pl.BlockSpec((None, None, 1, mod.shape[-1]), lambda i: (l, jnp.minimum(i // per_b, nb), 0, 0))],
        out_specs=pl.BlockSpec((tm, d), lambda i: (i, 0)),
        out_shape=jax.ShapeDtypeStruct((n_rows, d), F32),
        compiler_params=pltpu.CompilerParams(dimension_semantics=("arbitrary",), vmem_limit_bytes=VMEM_LIMIT),
        name="moe_combine",
    )(x1, yf, mod)


def _rope_tables(s, pad):
    rows = s // GRID_W
    row = jnp.repeat(jnp.arange(rows, dtype=jnp.int32), GRID_W).astype(F32)
    col = jnp.tile(jnp.arange(GRID_W, dtype=jnp.int32), rows).astype(F32)
    inv = ROPE_THETA ** (-jnp.arange(N_FREQ, dtype=F32) / N_FREQ)
    ang_r, ang_c = row[:, None] * inv, col[:, None] * inv
    cr, sr, cc, sc = jnp.cos(ang_r), jnp.sin(ang_r), jnp.cos(ang_c), jnp.sin(ang_c)
    one, zero = jnp.ones((pad, N_FREQ), F32), jnp.zeros((pad, N_FREQ), F32)
    cs_t = jnp.concatenate([jnp.concatenate([cr, sr, cc, sc], axis=1),
                            jnp.concatenate([one, zero, one, zero], axis=1)], axis=0).T
    cpat = jnp.concatenate([cr, cr, cc, cc], axis=1)
    spat = jnp.concatenate([-sr, sr, -sc, sc], axis=1)
    ident = jnp.concatenate([jnp.ones((pad, 2 * HEAD_DIM), F32), jnp.zeros((pad, 2 * HEAD_DIM), F32)], axis=1)
    cs_nat = jnp.concatenate([jnp.concatenate([cpat, cpat, spat, spat], axis=1), ident], axis=0)
    return cs_nat, cs_t


def kernel(x, c, ctx, c_ctx, ada_w, ada_b, attn_norm_g, ffn_norm_g, w_in, qn_global, kn_global, qn_local, kn_local,
           sink_logit, w_out, router_w, router_b, w_gate, w_up, w_down):
    nb, s, d = x.shape
    n_ctx = ctx.shape[1]
    depth = ada_w.shape[0]
    tm = TOKEN_TILE
    assert s % tm == 0 and nb * n_ctx == tm and s % GRID_W == 0 and nb + 1 <= MOD_ROWS
    dims = dict(batch=nb, seq=s, n_ctx=n_ctx, tiles_per_batch=s // tm, n_lat_tiles=nb * s // tm)

    cvec = jnp.concatenate([c, c_ctx[None, :], jnp.zeros((MOD_ROWS - nb - 1, d), F32)], axis=0)
    mod = _modulation(cvec, ada_w, ada_b)[:, :nb + 1].reshape(depth, nb + 1, 1, 6 * d)

    cs_nat, cs_t = _rope_tables(s, tm)
    head_of_lane = np.arange(LANES) // HEAD_DIM
    block_ones = jnp.asarray(head_of_lane[:, None] == head_of_lane[None, :], BF16)
    w_in_b = w_in.astype(BF16)
    w_out_b = w_out.astype(BF16)
    wg_b, wu_b, wd_b = w_gate.astype(BF16), w_up.astype(BF16), w_down.astype(BF16)
    rw_t = router_w.T.astype(BF16)
    rb_col = router_b.reshape(N_EXPERTS, 1).astype(F32)

    stream = (jnp.concatenate([x.reshape(nb * s, d), ctx.reshape(nb * n_ctx, d)], axis=0),)
    tq_g, tk_g = min(512, s), min(4096, s)
    tq_l = 256
    for l in range(depth):
        gq_cols = jnp.stack([qn_global[l], qn_local[l]]).reshape(2, HEAD_DIM, 1)
        gk_rows = jnp.stack([jnp.tile(kn_global[l], KV_HEADS), jnp.tile(kn_local[l], KV_HEADS)]).reshape(2, 1, LANES)
        outs = _pre_attention(
            stream, mod, l, attn_norm_g[l].reshape(1, d), w_in_b[l], gq_cols, gk_rows, block_ones, cs_nat, cs_t, dims)
        qgt, kg, vgt, qlt, kl, vlt = outs[:6]
        x_all = outs[6] if len(outs) > 6 else stream[0]
        ygt = _global_attention(qgt, kg, vgt, dims, tq_g, tk_g)
        ylt = _local_attention(sink_logit[l], qlt, kl, vlt, dims, tq_l)
        yct = _ctx_attention(sink_logit[l], qgt, kg, vgt, qlt, kl, vlt, dims)
        x1, hp, meta = _post_attention(ygt, ylt, yct, x_all, mod, l, w_out_b[l], ffn_norm_g[l].reshape(1, d),
                                        rw_t, rb_col, dims)
        stream = (x1, _moe(hp, meta, wg_b, wu_b, wd_b, l))
    return _combine(stream[0], stream[1], mod, depth - 1, nb * s, dims).reshape(nb, s, d)
```

```python
import functools
import math

import numpy as np
import jax
import jax.numpy as jnp
from jax import lax
from jax.experimental import pallas as pl
from jax.experimental.pallas import tpu as pltpu

F32 = jnp.float32
BF16 = jnp.bfloat16
FP8 = jnp.float8_e4m3fn

HEAD_DIM = 64
GRID_W = 64
WINDOW = 128
N_HEADS_MIXER = 8
KV_HEADS = 2
GROUP = N_HEADS_MIXER // KV_HEADS
ROPE_THETA = 10000.0
N_EXPERTS = 16
N_GROUPS = 4
EXPERTS_PER_GROUP = N_EXPERTS // N_GROUPS
EXPERT_PAIRS = tuple((a, b) for a in range(EXPERTS_PER_GROUP) for b in range(a + 1, EXPERTS_PER_GROUP))
N_CLASSES = N_GROUPS * len(EXPERT_PAIRS)
NORM_EPS = 1e-6
ATTN_SCALE = 1.0 / math.sqrt(HEAD_DIM)
LOG2_E = 1.4426950408889634
Q_SCALE = ATTN_SCALE * LOG2_E
NEG_INF = -1e30
N_FREQ = HEAD_DIM // 4

LANES = 128
TOKEN_TILE = 512
TOKEN_ROWS = 8
MOE_TILE = 256
QK_DEPTH = 256
R_ROW = 3 * HEAD_DIM
R_TILE = 32
R_MAX = 256.0
P_SHIFT = 3.0
P_LIMIT = 256.0
ONES_ROWS = 16
V_ROWS = HEAD_DIM + ONES_ROWS
VMEM_LIMIT = 48 * 1024 * 1024
MOD_ROWS = 16


def _dot(a, b):
    return jnp.dot(a, b, preferred_element_type=F32)


def _sigmoid(x):
    return 1.0 / (1.0 + jnp.exp(-x))


def _mod_kernel(c_ref, w_ref, b_ref, o_ref):
    c = c_ref[...]
    s = (c * _sigmoid(c)).astype(BF16)
    o_ref[0] = _dot(s, w_ref[0].astype(BF16)) + b_ref[0]


def _modulation(cvec, ada_w, ada_b):
    depth, d, d6 = ada_w.shape
    col = 1536
    assert d6 % col == 0
    return pl.pallas_call(
        _mod_kernel,
        grid=(depth, d6 // col),
        in_specs=[pl.BlockSpec((MOD_ROWS, d), lambda l, j: (0, 0)),
                  pl.BlockSpec((1, d, col), lambda l, j: (l, 0, j)),
                  pl.BlockSpec((1, 1, col), lambda l, j: (l, 0, j))],
        out_specs=pl.BlockSpec((1, MOD_ROWS, col), lambda l, j: (l, 0, j)),
        out_shape=jax.ShapeDtypeStruct((depth, MOD_ROWS, d6), F32),
        compiler_params=pltpu.CompilerParams(dimension_semantics=("arbitrary", "arbitrary"),
                                             vmem_limit_bytes=VMEM_LIMIT),
        name="modulation",
    )(cvec, ada_w, ada_b.reshape(depth, 1, d6))


def _rms_rows(x, gain):
    ms = jnp.mean(x * x, axis=-1, keepdims=True)
    return x * lax.rsqrt(ms + NORM_EPS) * gain


def _rope_t(t, cs_t):
    n = N_FREQ
    cr, sr, cc, sc = cs_t[0:n], cs_t[n:2 * n], cs_t[2 * n:3 * n], cs_t[3 * n:4 * n]
    x1r, x2r, x1c, x2c = t[0:n], t[n:2 * n], t[2 * n:3 * n], t[3 * n:4 * n]
    return jnp.concatenate([x1r * cr - x2r * sr, x2r * cr + x1r * sr,
                            x1c * cc - x2c * sc, x2c * cc + x1c * sc], axis=0)


def _q_heads_t(y_pair, gain_col, cs_t, q_scale):
    t = y_pair.T
    outs = []
    for j in range(2):
        th = t[j * HEAD_DIM:(j + 1) * HEAD_DIM]
        ms = jnp.mean(th * th, axis=0, keepdims=True)
        thn = th * lax.rsqrt(ms + NORM_EPS) * gain_col
        outs.append(_rope_t(thn, cs_t) * q_scale)
    return outs


def _split_fp8(x):
    hi = x.astype(FP8).astype(F32)
    lo = (x - hi).astype(FP8).astype(F32)
    return hi, lo


def _k_heads(kc, block_ones, gain_row, cs):
    sq = kc * kc
    hi = sq.astype(BF16)
    lo = (sq - hi.astype(F32)).astype(BF16)
    ss = _dot(hi, block_ones) + _dot(lo, block_ones)
    kn = kc * lax.rsqrt(ss * (1.0 / HEAD_DIM) + NORM_EPS) * gain_row
    lane = lax.broadcasted_iota(jnp.int32, kn.shape, 1)
    first_half = (lane % (2 * N_FREQ)) < N_FREQ
    swapped = jnp.where(first_half, pltpu.roll(kn, LANES - N_FREQ, 1), pltpu.roll(kn, N_FREQ, 1))
    return kn * cs[:, :LANES] + swapped * cs[:, LANES:]


def _pre_kernel(*refs, combine):
    if combine:
        x1_ref, yf_ref, modp_ref = refs[:3]
        refs = refs[3:]
        xo_ref = refs[-1]
        refs = refs[:-1]
        tm, d = x1_ref.shape
        x = x1_ref[...] + modp_ref[:, 5 * d:6 * d] * _from_token_tiles(yf_ref, tm, d // LANES)
        xo_ref[...] = x
    else:
        x = refs[0][...]
        refs = refs[1:]
    (mod_ref, g_ref, w_ref, gq_ref, gk_ref, bones_ref, cs_ref, cst_ref,
     qgt_ref, kg_ref, vgt_ref, qlt_ref, kl_ref, vlt_ref) = refs
    tm, d = x.shape
    shift = mod_ref[:, 0:d]
    scale = mod_ref[:, d:2 * d]
    h = _rms_rows(x, g_ref[...]) * (1.0 + scale) + shift
    y = _dot(h.astype(BF16), w_ref[...])
    cs_t = cst_ref[...]
    cs = cs_ref[...]
    zeros = jnp.zeros((HEAD_DIM, tm), BF16)
    qw = N_HEADS_MIXER * HEAD_DIM
    kw = KV_HEADS * HEAD_DIM
    for mixer, (qt_ref, k_ref, vt_ref) in enumerate(((qgt_ref, kg_ref, vgt_ref), (qlt_ref, kl_ref, vlt_ref))):
        base = mixer * (qw + 2 * kw)
        gain_col = gq_ref[mixer]
        split = mixer == 0
        for c in range(N_HEADS_MIXER // 2):
            pair = _q_heads_t(y[:, base + c * LANES: base + (c + 1) * LANES], gain_col, cs_t, Q_SCALE)
            for j in range(2):
                if split:
                    hi, lo = _split_fp8(pair[j])
                    qt_ref[2 * c + j] = jnp.concatenate([hi, hi, lo, jnp.zeros_like(hi)], axis=0).astype(FP8)
                else:
                    qt_ref[2 * c + j] = jnp.concatenate([pair[j].astype(BF16), zeros], axis=0)
        kc = y[:, base + qw: base + qw + kw]
        kboth = _k_heads(kc, bones_ref[...], gk_ref[mixer], cs)
        lane = lax.broadcasted_iota(jnp.int32, kboth.shape, 1)
        low = lane < HEAD_DIM
        ones_col = jnp.where(lane == HEAD_DIM, 1.0, 0.0)
        if split:
            hi, lo = _split_fp8(kboth)
            hi_sw, lo_sw = pltpu.roll(hi, HEAD_DIM, 1), pltpu.roll(lo, HEAD_DIM, 1)
            k_ref[0] = jnp.concatenate([jnp.where(low, hi, lo_sw), jnp.where(low, hi, ones_col)], axis=1).astype(FP8)
            k_ref[1] = jnp.concatenate([jnp.where(low, hi_sw, lo), jnp.where(low, hi_sw, ones_col)], axis=1).astype(FP8)
        else:
            k_ref[0] = jnp.where(low, kboth, ones_col).astype(BF16)
            k_ref[1] = jnp.where(low, pltpu.roll(kboth, HEAD_DIM, 1), ones_col).astype(BF16)
        vc = y[:, base + qw + kw: base + qw + 2 * kw]
        vt = vc.T.astype(BF16)
        for kv in range(KV_HEADS):
            vt_ref[kv] = vt[kv * HEAD_DIM:(kv + 1) * HEAD_DIM]


def _pre_attention(stream, mod, l, attn_g, w_in_b, gq_cols, gk_rows, block_ones, cs_nat, cs_t, dims):
    combine = len(stream) == 2
    ntok, d = stream[0].shape
    tm = TOKEN_TILE
    n_lat, per_b, nb = dims["n_lat_tiles"], dims["tiles_per_batch"], dims["batch"]
    n_tiles = ntok // tm

    def mod_idx(i):
        return (l, jnp.minimum(i // per_b, nb), 0, 0)

    def tab_idx(i):
        return jnp.where(i < n_lat, i % per_b, per_b)

    qt_shape = jax.ShapeDtypeStruct((N_HEADS_MIXER, LANES, ntok), BF16)
    k_shape = jax.ShapeDtypeStruct((KV_HEADS, ntok, LANES), BF16)
    vt_shape = jax.ShapeDtypeStruct((KV_HEADS, HEAD_DIM, ntok), BF16)
    qt_spec = pl.BlockSpec((N_HEADS_MIXER, LANES, tm), lambda i: (0, 0, i))
    k_spec = pl.BlockSpec((KV_HEADS, tm, LANES), lambda i: (0, i, 0))
    vt_spec = pl.BlockSpec((KV_HEADS, HEAD_DIM, tm), lambda i: (0, 0, i))
    qt8_shape = jax.ShapeDtypeStruct((N_HEADS_MIXER, QK_DEPTH, ntok), FP8)
    k8_shape = jax.ShapeDtypeStruct((KV_HEADS, ntok, QK_DEPTH), FP8)
    qt8_spec = pl.BlockSpec((N_HEADS_MIXER, QK_DEPTH, tm), lambda i: (0, 0, i))
    k8_spec = pl.BlockSpec((KV_HEADS, tm, QK_DEPTH), lambda i: (0, i, 0))
    row_spec =pl.BlockSpec((tm, d), lambda i: (i, 0))
    mod_spec = pl.BlockSpec((None, None, 1, mod.shape[-1]), mod_idx)
    if combine:
        modp_spec = pl.BlockSpec((None, None, 1, mod.shape[-1]), lambda i: (l - 1,) + mod_idx(i)[1:])
        tile_spec = pl.BlockSpec((tm * TOKEN_ROWS, LANES), lambda i: (i, 0))
        stream_specs, stream_args = [row_spec, tile_spec, modp_spec], [stream[0], stream[1], mod]
        extra_specs, extra_shapes = [row_spec], [jax.ShapeDtypeStruct((ntok, d), F32)]
    else:
        stream_specs, stream_args, extra_specs, extra_shapes = [row_spec], [stream[0]], [], []
    return pl.pallas_call(
        functools.partial(_pre_kernel, combine=combine),
        grid=(n_tiles,),
        in_specs=stream_specs + [
                  mod_spec,
                  pl.BlockSpec((1, d), lambda i: (0, 0)),
                  pl.BlockSpec(w_in_b.shape, lambda i: (0, 0)),
                  pl.BlockSpec(gq_cols.shape, lambda i: (0, 0, 0)),
                  pl.BlockSpec(gk_rows.shape, lambda i: (0, 0, 0)),
                  pl.BlockSpec(block_ones.shape, lambda i: (0, 0)),
                  pl.BlockSpec((tm, 2 * LANES), lambda i: (tab_idx(i), 0)),
                  pl.BlockSpec((HEAD_DIM, tm), lambda i: (0, tab_idx(i)))],
        out_specs=[qt8_spec, k8_spec, vt_spec, qt_spec, k_spec, vt_spec] + extra_specs,
        out_shape=[qt8_shape, k8_shape, vt_shape, qt_shape, k_shape, vt_shape] + extra_shapes,
        compiler_params=pltpu.CompilerParams(dimension_semantics=("arbitrary",), vmem_limit_bytes=VMEM_LIMIT),
        name="pre_attention",
    )(*stream_args, mod, attn_g, w_in_b, gq_cols, gk_rows, block_ones, cs_nat, cs_t)


def _with_ones(vt):
    return jnp.concatenate([vt, jnp.ones((ONES_ROWS, vt.shape[1]), BF16)], axis=0)


def _round_ref(x):
    inside = jnp.abs(x) <= R_MAX
    return jnp.where(inside, jnp.clip(x, -R_MAX, R_MAX).astype(FP8).astype(F32), x)


def _flash_kernel(q_ref, k_ref, v_ref, kc_ref, vc_ref, o_ref, r_scr, acc_scr, new_scr, pmax_scr):
    ki = pl.program_id(3)
    tq = q_ref.shape[2]
    zpad = jnp.zeros((QK_DEPTH - R_ROW - R_TILE, tq), FP8)

    def q_ext(g, r_row):
        neg = jnp.concatenate([-r_row, jnp.zeros((R_TILE - 1, tq), F32)], axis=0).astype(FP8)
        return jnp.concatenate([q_ref[g, 0:R_ROW], neg, zpad], axis=0)

    def two_pass_update(g, k, v1, first):
        r_old = jnp.zeros((1, tq), F32) if first else r_scr[g]
        st = _dot(k, q_ref[g]) - r_old
        top = jnp.max(st, axis=0, keepdims=True) - P_SHIFT
        delta = _round_ref(r_old + (top if first else jnp.maximum(top, 0.0))) - r_old
        p = jnp.exp2(st - delta).astype(BF16)
        pv = _dot(v1, p)
        acc_scr[g] = pv if first else acc_scr[g] * jnp.exp2(-delta) + pv
        r_scr[g] = r_old + delta

    @pl.when(ki == 0)
    def _():
        vc1 = _with_ones(vc_ref[...])
        kc = kc_ref[...]
        for g in range(GROUP):
            two_pass_update(g, kc, vc1, True)

    v1 = _with_ones(v_ref[...])
    v8 = v1.astype(FP8)
    k = k_ref[...]
    for g in range(GROUP):
        st = _dot(k, q_ext(g, r_scr[g]))
        p = jnp.exp2(st.astype(BF16))
        pmax_scr[g] = jnp.max(p, axis=0, keepdims=True).astype(F32)
        new_scr[g] = acc_scr[g] + _dot(v8, p.astype(FP8))
    clean = (jnp.max(pmax_scr[...]) <= P_LIMIT) & (jnp.max(jnp.abs(r_scr[...])) <= R_MAX)

    @pl.when(clean)
    def _():
        for g in range(GROUP):
            r_old = r_scr[g]
            r_new = _round_ref(r_old + jnp.maximum(jnp.log2(pmax_scr[g]) - P_SHIFT, 0.0))
            acc_scr[g] = new_scr[g] * jnp.exp2(r_old - r_new)
            r_scr[g] = r_new

    @pl.when(jnp.logical_not(clean))
    def _():
        for g in range(GROUP):
            two_pass_update(g, k, v1, False)

    @pl.when(ki == pl.num_programs(3) - 1)
    def _():
        for g in range(GROUP):
            a = acc_scr[g]
            o_ref[g * HEAD_DIM:(g + 1) * HEAD_DIM, :] = (a[0:HEAD_DIM] / a[HEAD_DIM:HEAD_DIM + 1]).astype(BF16)


def _global_attention(qt, k, vt, dims, tq, tk):
    nb, s, n_ctx = dims["batch"], dims["seq"], dims["n_ctx"]
    nq, nk = s // tq, s // tk
    ctx_blk0 = (nb * s) // n_ctx
    return pl.pallas_call(
        _flash_kernel,
        grid=(nb, KV_HEADS, nq, nk),
        in_specs=[pl.BlockSpec((GROUP, QK_DEPTH, tq), lambda b, kv, qi, ki: (kv, 0, b * nq + qi)),
                  pl.BlockSpec((None, tk, QK_DEPTH), lambda b, kv, qi, ki: (kv, b * nk + ki, 0)),
                  pl.BlockSpec((None, HEAD_DIM, tk), lambda b, kv, qi, ki: (kv, 0, b * nk + ki)),
                  pl.BlockSpec((None, n_ctx, QK_DEPTH), lambda b, kv, qi, ki: (kv, ctx_blk0 + b, 0)),
                  pl.BlockSpec((None, HEAD_DIM, n_ctx), lambda b, kv, qi, ki: (kv, 0, ctx_blk0 + b))],
        out_specs=pl.BlockSpec((GROUP * HEAD_DIM, tq), lambda b, kv, qi, ki: (kv, b * nq + qi)),
        out_shape=jax.ShapeDtypeStruct((N_HEADS_MIXER * HEAD_DIM, nb * s), BF16),
        scratch_shapes=[pltpu.VMEM((GROUP, 1, tq), F32), pltpu.VMEM((GROUP, V_ROWS, tq), F32),
                        pltpu.VMEM((GROUP, V_ROWS, tq), F32), pltpu.VMEM((GROUP, 1, tq), F32)],
        compiler_params=pltpu.CompilerParams(
            dimension_semantics=("arbitrary", "arbitrary", "arbitrary", "arbitrary"), vmem_limit_bytes=VMEM_LIMIT),
        name="global_attention",
    )(qt, k, vt, k, vt)


def _single_pass_attend(k_all, v1_all, q_t, bias, sink):
    st = _dot(k_all, q_t)
    if bias is not None:
        st = st + bias
    m = jnp.max(st, axis=0, keepdims=True)
    if sink is not None:
        m = jnp.maximum(m, sink)
    p = jnp.exp2((st - m).astype(BF16))
    a = _dot(v1_all, p)
    den = a[HEAD_DIM:HEAD_DIM + 1]
    if sink is not None:
        den = den + jnp.exp2(sink - m)
    return a[0:HEAD_DIM] / den


def _band_bias(tq, n_ctx):
    span = tq + 2 * WINDOW
    j = np.arange(span + n_ctx)[:, None]
    i = np.arange(tq)[None, :]
    band = (np.abs(j - WINDOW - i) <= WINDOW) | (j >= span)
    out = []
    for case in range(4):
        first, last = case & 1, case & 2
        ok = band & ~((j < WINDOW) & bool(first)) & ~((j >= WINDOW + tq) & (j < span) & bool(last))
        out.append(np.where(ok, 0.0, NEG_INF))
    return jnp.asarray(np.stack(out), F32)


def _local_kernel(sink_ref, q_ref, kp_ref, km_ref, kn_ref, kc_ref, vp_ref, vm_ref, vn_ref, vc_ref, bias_ref, o_ref):
    kv = pl.program_id(1)
    qi = pl.program_id(2)
    tq = q_ref.shape[2]
    case = jnp.where(qi == 0, 1, 0) + jnp.where(qi == pl.num_programs(2) - 1, 2, 0)
    bias = bias_ref[case]
    k_all = jnp.concatenate([kp_ref[...], km_ref[...], kn_ref[...], kc_ref[...]], axis=0)
    v1_all = _with_ones(jnp.concatenate([vp_ref[...], vm_ref[...], vn_ref[...], vc_ref[...]], axis=1))
    q_all = jnp.concatenate([q_ref[g] for g in range(GROUP)], axis=1)
    sink = jnp.concatenate([jnp.full((1, tq), sink_ref[kv * GROUP + g] * LOG2_E, F32) for g in range(GROUP)], axis=1)
    o = _single_pass_attend(k_all, v1_all, q_all, jnp.concatenate([bias] * GROUP, axis=1), sink)
    for g in range(GROUP):
        o_ref[g * HEAD_DIM:(g + 1) * HEAD_DIM, :] = o[:, g * tq:(g + 1) * tq].astype(BF16)


def _local_attention(sink, qt, k, vt, dims, tq):
    nb, s, n_ctx = dims["batch"], dims["seq"], dims["n_ctx"]
    nq = s // tq
    w = WINDOW
    per_b_w = s // w
    r = tq // w
    ctx_blk0 = (nb * s) // n_ctx
    bias = _band_bias(tq, n_ctx)

    def prev_idx(b, qi):
        return b * per_b_w + jnp.maximum(qi * r - 1, 0)

    def next_idx(b, qi):
        return b * per_b_w + jnp.minimum(qi * r + r, per_b_w - 1)

    return pl.pallas_call(
        _local_kernel,
        grid=(nb, KV_HEADS, nq),
        in_specs=[pl.BlockSpec(memory_space=pltpu.SMEM),
                  pl.BlockSpec((GROUP, LANES, tq), lambda b, kv, qi: (kv, 0, b * nq + qi)),
                  pl.BlockSpec((None, w, LANES), lambda b, kv, qi: (kv, prev_idx(b, qi), 0)),
                  pl.BlockSpec((None, tq, LANES), lambda b, kv, qi: (kv, b * nq + qi, 0)),
                  pl.BlockSpec((None, w, LANES), lambda b, kv, qi: (kv, next_idx(b, qi), 0)),
                  pl.BlockSpec((None, n_ctx, LANES), lambda b, kv, qi: (kv, ctx_blk0 + b, 0)),
                  pl.BlockSpec((None, HEAD_DIM, w), lambda b, kv, qi: (kv, 0, prev_idx(b, qi))),
                  pl.BlockSpec((None, HEAD_DIM, tq), lambda b, kv, qi: (kv, 0, b * nq + qi)),
                  pl.BlockSpec((None, HEAD_DIM, w), lambda b, kv, qi: (kv, 0, next_idx(b, qi))),
                  pl.BlockSpec((None, HEAD_DIM, n_ctx), lambda b, kv, qi: (kv, 0, ctx_blk0 + b)),
                  pl.BlockSpec(bias.shape, lambda b, kv, qi: (0, 0, 0))],
        out_specs=pl.BlockSpec((GROUP * HEAD_DIM, tq), lambda b, kv, qi: (kv, b * nq + qi)),
        out_shape=jax.ShapeDtypeStruct((N_HEADS_MIXER * HEAD_DIM, nb * s), BF16),
        compiler_params=pltpu.CompilerParams(
            dimension_semantics=("arbitrary", "arbitrary", "arbitrary"), vmem_limit_bytes=VMEM_LIMIT),
        name="local_attention",
    )(sink, qt, k, k, k, k, vt, vt, vt, vt, bias)


def _ctx_kernel(sink_ref, qg_ref, kg_ref, vg_ref, ql_ref, kl_ref, vl_ref, o_ref):
    tq = qg_ref.shape[2]
    for mixer, (q_ref, k_ref, v_ref) in enumerate(((qg_ref, kg_ref, vg_ref), (ql_ref, kl_ref, vl_ref))):
        for kv in range(KV_HEADS):
            k_all = k_ref[kv]
            v1 = _with_ones(v_ref[kv])
            for g in range(GROUP):
                hq = kv * GROUP + g
                sink = jnp.full((1, tq), sink_ref[hq] * LOG2_E, F32) if mixer == 1 else None
                o = _single_pass_attend(k_all, v1, q_ref[hq], None, sink)
                row = (mixer * N_HEADS_MIXER + hq) * HEAD_DIM
                o_ref[row:row + HEAD_DIM, :] = o.astype(BF16)


def _ctx_attention(sink, qgt, kg, vgt, qlt, kl, vlt, dims):
    nb, s, n_ctx = dims["batch"], dims["seq"], dims["n_ctx"]
    ctx_blk0 = (nb * s) // n_ctx
    def q_spec(depth):
        return pl.BlockSpec((N_HEADS_MIXER, depth, n_ctx), lambda b: (0, 0, ctx_blk0 + b))

    def k_spec(depth):
        return pl.BlockSpec((KV_HEADS, n_ctx, depth), lambda b: (0, ctx_blk0 + b, 0))

    v_spec = pl.BlockSpec((KV_HEADS, HEAD_DIM, n_ctx), lambda b: (0, 0, ctx_blk0 + b))
    rows = 2 * N_HEADS_MIXER * HEAD_DIM
    return pl.pallas_call(
        _ctx_kernel,
        grid=(nb,),
        in_specs=[pl.BlockSpec(memory_space=pltpu.SMEM), q_spec(QK_DEPTH), k_spec(QK_DEPTH), v_spec,
                  q_spec(LANES), k_spec(LANES), v_spec],
        out_specs=pl.BlockSpec((rows, n_ctx), lambda b: (0, b)),
        out_shape=jax.ShapeDtypeStruct((rows, nb * n_ctx), BF16),
        compiler_params=pltpu.CompilerParams(dimension_semantics=("arbitrary",), vmem_limit_bytes=VMEM_LIMIT),
        name="ctx_attention",
    )(sink, qgt, kg, vgt, qlt, kl, vlt)


def _to_token_tiles(ref, mat):
    t = mat.shape[0]
    for j in range(mat.shape[1] // LANES):
        ref[pl.ds(j, t, stride=TOKEN_ROWS), :] = mat[:, j * LANES:(j + 1) * LANES]


def _from_token_tiles(ref, t, n_chunks):
    return jnp.concatenate([ref[pl.ds(j, t, stride=TOKEN_ROWS), :] for j in range(n_chunks)], axis=1)


def _route(sel, scores):
    n = EXPERTS_PER_GROUP
    group_score = []
    for g in range(N_GROUPS):
        r = sel[g * n:(g + 1) * n]
        best = None
        for a in range(n):
            for b in range(a + 1, n):
                pair = r[a] + r[b]
                best = pair if best is None else jnp.maximum(best, pair)
        group_score.append(best)
    gates = [None] * N_EXPERTS
    picked = []
    for g in range(N_GROUPS):
        chosen = None
        for h in range(N_GROUPS):
            if h == g:
                continue
            c = (group_score[g] > group_score[h]) if h < g else (group_score[g] >= group_score[h])
            chosen = c if chosen is None else (chosen & c)
        for a in range(n):
            ea = g * n + a
            rank = jnp.zeros_like(sel[ea])
            for b in range(n):
                if b == a:
                    continue
                eb = g * n + b
                ahead = (sel[eb] >= sel[ea]) if b < a else (sel[eb] > sel[ea])
                rank = rank + jnp.where(ahead, 1.0, 0.0)
            picked.append(chosen & (rank < 2.0))
    den = None
    for e in range(N_EXPERTS):
        w = jnp.where(picked[e], scores[e], 0.0)
        gates[e] = w
        den = w if den is None else den + w
    cls = jnp.zeros_like(den)
    for g in range(N_GROUPS):
        for code, (a, b) in enumerate(EXPERT_PAIRS):
            both = picked[g * n + a] & picked[g * n + b]
            cls = cls + jnp.where(both, float(g * len(EXPERT_PAIRS) + code), 0.0)
    return [w / den for w in gates], cls


def _post_kernel(yg_ref, yl_ref, yc_ref, x_ref, mod_ref, wout_ref, g_ref, rwt_ref, rb_ref,
                 x1_ref, hp_ref, meta_ref, *, n_lat):
    i = pl.program_id(0)
    d = x_ref.shape[1]
    tm = x_ref.shape[0]
    yt_lat = jnp.concatenate([yg_ref[...], yl_ref[...]], axis=0)
    is_lat = jnp.full(yt_lat.shape, i, jnp.int32) < n_lat
    yt = jnp.where(is_lat, yt_lat, yc_ref[...])
    a = lax.dot_general(yt, wout_ref[...], (((0,), (0,)), ((), ())), preferred_element_type=F32)
    gate1 = mod_ref[:, 2 * d:3 * d]
    shift2 = mod_ref[:, 3 * d:4 * d]
    scale2 = mod_ref[:, 4 * d:5 * d]
    x1 = x_ref[...] + gate1 * a
    x1_ref[...] = x1
    h2 = _rms_rows(x1, g_ref[...]) * (1.0 + scale2) + shift2
    logits_t = lax.dot_general(rwt_ref[...], h2.astype(BF16), (((1,), (1,)), ((), ())),
                               preferred_element_type=F32)
    scores_t = _sigmoid(logits_t)
    sel_t = scores_t + rb_ref[...]
    gates, cls = _route([sel_t[e:e + 1] for e in range(N_EXPERTS)], [scores_t[e:e + 1] for e in range(N_EXPERTS)])
    gates_t = jnp.concatenate(gates + [jnp.zeros((LANES - N_EXPERTS, tm), F32)], axis=0)
    bits = lax.bitcast_convert_type(h2.astype(BF16).astype(F32), jnp.uint32)
    half = d // 2
    words = (bits[:, :half] & jnp.uint32(0xFFFF0000)) | (bits[:, half:] >> 16)
    spare = jnp.zeros((tm, TOKEN_ROWS * LANES - half - LANES), jnp.uint32)
    _to_token_tiles(hp_ref, jnp.concatenate([words, lax.bitcast_convert_type(gates_t.T, jnp.uint32), spare], axis=1))
    meta_ref[...] = jnp.concatenate([cls, jnp.zeros((7, tm), F32)], axis=0)


def _post_attention(ygt, ylt, yct, x_all, mod, l, w_out_b, ffn_g, rw_t, rb_col, dims):
    ntok, d = x_all.shape
    tm = TOKEN_TILE
    n_lat, per_b, nb = dims["n_lat_tiles"], dims["tiles_per_batch"], dims["batch"]
    rows = ygt.shape[0]

    def mod_idx(i):
        return (l, jnp.minimum(i // per_b, nb), 0, 0)

    def lat_idx(i):
        return (0, jnp.minimum(i, n_lat - 1))

    return pl.pallas_call(
        functools.partial(_post_kernel, n_lat=n_lat),
        grid=(ntok // tm,),
        in_specs=[pl.BlockSpec((rows, tm), lat_idx),
                  pl.BlockSpec((rows, tm), lat_idx),
                  pl.BlockSpec((2 * rows, tm), lambda i: (0, 0)),
                  pl.BlockSpec((tm, d), lambda i: (i, 0)),
                  pl.BlockSpec((None, None, 1, mod.shape[-1]), mod_idx),
                  pl.BlockSpec(w_out_b.shape, lambda i: (0, 0)),
                  pl.BlockSpec((1, d), lambda i: (0, 0)),
                  pl.BlockSpec(rw_t.shape, lambda i: (0, 0)),
                  pl.BlockSpec(rb_col.shape, lambda i: (0, 0))],
        out_specs=[pl.BlockSpec((tm, d), lambda i: (i, 0)),
                   pl.BlockSpec((tm * TOKEN_ROWS, LANES), lambda i: (i, 0)),
                   pl.BlockSpec((8, tm), lambda i: (0, i))],
        out_shape=[jax.ShapeDtypeStruct((ntok, d), F32),
                   jax.ShapeDtypeStruct((ntok * TOKEN_ROWS, LANES), jnp.uint32),
                   jax.ShapeDtypeStruct((8, ntok), F32)],
        compiler_params=pltpu.CompilerParams(dimension_semantics=("arbitrary",), vmem_limit_bytes=VMEM_LIMIT),
        name="post_attention",
    )(ygt, ylt, yct, x_all, mod, w_out_b, ffn_g, rw_t, rb_col)


def _sort_plan(cls, tm):
    ntok = cls.shape[0]
    p_tiles = -(-ntok // tm) + N_CLASSES
    order = jnp.argsort(cls, stable=True).astype(jnp.int32)
    classes = jnp.arange(N_CLASSES, dtype=jnp.int32)
    counts = jnp.sum((cls[:, None] == classes[None, :]).astype(jnp.int32), axis=0)
    tiles_per = (counts + tm - 1) // tm
    tile_end = jnp.cumsum(tiles_per)
    start = jnp.cumsum(counts) - counts
    tile = jnp.arange(p_tiles, dtype=jnp.int32)
    tile_cls = jnp.minimum(jnp.sum((tile_end[None, :] <= tile[:, None]).astype(jnp.int32), axis=1), N_CLASSES - 1)
    is_cls = tile_cls[:, None] == classes[None, :]

    def per_tile(v):
        return jnp.sum(jnp.where(is_cls, v[None, :], 0), axis=1)

    row0 = (tile - per_tile(tile_end - tiles_per)) * tm
    left = jnp.where(tile < tile_end[N_CLASSES - 1], per_tile(counts) - row0, 0)
    p = p_tiles * tm
    pad_before = (tile_end - tiles_per) * tm - start
    base = jnp.concatenate([jnp.zeros((p,), jnp.int32), order, jnp.zeros((p,), jnp.int32)])
    token = jnp.zeros((p_tiles, tm), jnp.int32)
    for c in range(N_CLASSES):
        shifted = lax.dynamic_slice(base, (p - pad_before[c],), (p,)).reshape(p_tiles, tm)
        token = jnp.where(is_cls[:, c:c + 1], shifted, token)
    r = jnp.arange(tm, dtype=jnp.int32)[None, :]
    valid = r < left[:, None]
    src = jnp.where(valid, token, 0).astype(jnp.int32)
    dst = jnp.where(valid, token, ntok + (tile % 2)[:, None] * tm + r).astype(jnp.int32)
    pair = jnp.asarray(EXPERT_PAIRS, jnp.int32)[tile_cls % len(EXPERT_PAIRS)]
    base_e = (tile_cls // len(EXPERT_PAIRS)) * EXPERTS_PER_GROUP
    return dict(src=src.reshape(p_tiles, 1, tm), dst=dst.reshape(p_tiles, 1, tm), ea=base_e + pair[:, 0],
                eb=base_e + pair[:, 1], n_used=tile_end[N_CLASSES - 1:].astype(jnp.int32), p_tiles=p_tiles)


ROW_UNROLL = 8


def _row_copy_all(idx_ref, n_rows, make_copy):
    def body(i, carry):
        for j in range(ROW_UNROLL):
            r = i * ROW_UNROLL + j
            make_copy(idx_ref[0, 0, r], r).start(priority=j % 2)
        return carry
    lax.fori_loop(0, n_rows // ROW_UNROLL, body, 0)


def _token_tile(ref, index):
    return ref.at[pl.ds(pl.multiple_of(index * TOKEN_ROWS, TOKEN_ROWS), TOKEN_ROWS)]


def _expert_pair_kernel(ea_ref, eb_ref, nu_ref, src_ref, nxt_ref, dst_ref, hp_hbm,
                        wga_ref, wua_ref, wda_ref, wgb_ref, wub_ref, wdb_ref, yf_hbm, hbuf, ybuf, gsem, ssem):
    t = pl.program_id(0)
    n_used = nu_ref[0]
    tm = src_ref.shape[2]
    d = wda_ref.shape[3]
    slot = t % 2
    other = 1 - slot

    def gather(idx_ref, s):
        _row_copy_all(idx_ref, tm, lambda tok, r: pltpu.make_async_copy(
            _token_tile(hp_hbm, tok), _token_tile(hbuf.at[s], r), gsem.at[s]))

    def gather_done(s):
        pltpu.make_async_copy(hp_hbm.at[pl.ds(0, tm * TOKEN_ROWS)], hbuf.at[s], gsem.at[s]).wait()

    def scatter_done(s):
        pltpu.make_async_copy(ybuf.at[s], yf_hbm.at[pl.ds(0, tm * TOKEN_ROWS)], ssem.at[s]).wait()

    @pl.when(t < n_used)
    def _():
        @pl.when(t == 0)
        def _():
            gather(src_ref, 0)

        @pl.when(t + 1 < n_used)
        def _():
            gather(nxt_ref, other)

        gather_done(slot)

        @pl.when(t >= 2)
        def _():
            scatter_done(slot)

        tiles = hbuf.at[slot]
        half_chunks = d // (2 * LANES)
        words = _from_token_tiles(tiles, tm, half_chunks)
        h = jnp.concatenate([lax.bitcast_convert_type(words & jnp.uint32(0xFFFF0000), F32).astype(BF16),
                             lax.bitcast_convert_type(words << 16, F32).astype(BF16)], axis=1)
        gates = lax.bitcast_convert_type(tiles[pl.ds(half_chunks, tm, stride=TOKEN_ROWS), :], F32)
        lane = lax.broadcasted_iota(jnp.int32, gates.shape, 1)
        y = None
        for e_ref, wg_ref, wu_ref, wd_ref in ((ea_ref, wga_ref, wua_ref, wda_ref), (eb_ref, wgb_ref, wub_ref, wdb_ref)):
            gcol = jnp.sum(jnp.where(lane == e_ref[t], gates, 0.0), axis=1, keepdims=True)
            a = _dot(h, wg_ref[0, 0])
            u = _dot(h, wu_ref[0, 0])
            he = ((a * _sigmoid(a)) * u * gcol).astype(BF16)
            part = _dot(he, wd_ref[0, 0])
            y = part if y is None else y + part
        _to_token_tiles(ybuf.at[slot], y)
        _row_copy_all(dst_ref, tm, lambda tok, r: pltpu.make_async_copy(
            _token_tile(ybuf.at[slot], r), _token_tile(yf_hbm, tok), ssem.at[slot]))

        @pl.when(t == n_used - 1)
        def _():
            @pl.when(t >= 1)
            def _():
                scatter_done(other)

            scatter_done(slot)


def _moe(hp, meta, wg_b, wu_b, wd_b, l):
    ntok = hp.shape[0] // TOKEN_ROWS
    tm = MOE_TILE
    d = wd_b.shape[-1]
    f = wg_b.shape[-1]
    plan = _sort_plan(meta[0].astype(jnp.int32), tm)
    p_tiles = plan["p_tiles"]

    def w_spec(shape, which):
        return pl.BlockSpec((1, 1) + shape, lambda t, ea, eb, nu: (l, (ea, eb)[which][t], 0, 0))

    def idx_spec(ahead):
        return pl.BlockSpec((1, 1, tm), lambda t, ea, eb, nu: (jnp.minimum(t + ahead, p_tiles - 1), 0, 0),
                            memory_space=pltpu.SMEM)

    return pl.pallas_call(
        _expert_pair_kernel,
        grid_spec=pltpu.PrefetchScalarGridSpec(
            num_scalar_prefetch=3,
            grid=(p_tiles,),
            in_specs=[idx_spec(0), idx_spec(1), idx_spec(0), pl.BlockSpec(memory_space=pl.ANY),
                      w_spec((d, f), 0), w_spec((d, f), 0), w_spec((f, d), 0),
                      w_spec((d, f), 1), w_spec((d, f), 1), w_spec((f, d), 1)],
            out_specs=pl.BlockSpec(memory_space=pl.ANY),
            scratch_shapes=[pltpu.VMEM((2, tm * TOKEN_ROWS, LANES), jnp.uint32),
                            pltpu.VMEM((2, tm * TOKEN_ROWS, LANES), F32),
                            pltpu.SemaphoreType.DMA((2,)), pltpu.SemaphoreType.DMA((2,))]),
        out_shape=jax.ShapeDtypeStruct(((ntok + 2 * tm) * TOKEN_ROWS, LANES), F32),
        compiler_params=pltpu.CompilerParams(dimension_semantics=("arbitrary",), vmem_limit_bytes=VMEM_LIMIT),
        name="moe_expert_pairs",
    )(plan["ea"], plan["eb"], plan["n_used"], plan["src"], plan["src"], plan["dst"], hp,
      wg_b, wu_b, wd_b, wg_b, wu_b, wd_b)


def _combine_kernel(x1_ref, yf_ref, mod_ref, o_ref):
    tm, d = x1_ref.shape
    o_ref[...] = x1_ref[...] + mod_ref[:, 5 * d:6 * d] * _from_token_tiles(yf_ref, tm, d // LANES)


def _combine(x1, yf, mod, l, n_rows, dims):
    d = x1.shape[1]
    tm = TOKEN_TILE
    per_b, nb = dims["tiles_per_batch"], dims["batch"]
    return pl.pallas_call(
        _combine_kernel,
        grid=(n_rows // tm,),
        in_specs=[pl.BlockSpec((tm, d), lambda i: (i, 0)),
                  pl.BlockSpec((tm * TOKEN_ROWS, LANES), lambda i: (i, 0)),
                  pl.BlockSpec((None, None, 1, mod.shape[-1]), lambda i: (l, jnp.minimum(i // per_b, nb), 0, 0))],
        out_specs=pl.BlockSpec((tm, d), lambda i: (i, 0)),
        out_shape=jax.ShapeDtypeStruct((n_rows, d), F32),
        compiler_params=pltpu.CompilerParams(dimension_semantics=("arbitrary",), vmem_limit_bytes=VMEM_LIMIT),
        name="moe_combine",
    )(x1, yf, mod)


def _rope_tables(s, pad):
    rows = s // GRID_W
    row = jnp.repeat(jnp.arange(rows, dtype=jnp.int32), GRID_W).astype(F32)
    col = jnp.tile(jnp.arange(GRID_W, dtype=jnp.int32), rows).astype(F32)
    inv = ROPE_THETA ** (-jnp.arange(N_FREQ, dtype=F32) / N_FREQ)
    ang_r, ang_c = row[:, None] * inv, col[:, None] * inv
    cr, sr, cc, sc = jnp.cos(ang_r), jnp.sin(ang_r), jnp.cos(ang_c), jnp.sin(ang_c)
    one, zero = jnp.ones((pad, N_FREQ), F32), jnp.zeros((pad, N_FREQ), F32)
    cs_t = jnp.concatenate([jnp.concatenate([cr, sr, cc, sc], axis=1),
                            jnp.concatenate([one, zero, one, zero], axis=1)], axis=0).T
    cpat = jnp.concatenate([cr, cr, cc, cc], axis=1)
    spat = jnp.concatenate([-sr, sr, -sc, sc], axis=1)
    ident = jnp.concatenate([jnp.ones((pad, 2 * HEAD_DIM), F32), jnp.zeros((pad, 2 * HEAD_DIM), F32)], axis=1)
    cs_nat = jnp.concatenate([jnp.concatenate([cpat, cpat, spat, spat], axis=1), ident], axis=0)
    return cs_nat, cs_t


def kernel(x, c, ctx, c_ctx, ada_w, ada_b, attn_norm_g, ffn_norm_g, w_in, qn_global, kn_global, qn_local, kn_local,
           sink_logit, w_out, router_w, router_b, w_gate, w_up, w_down):
    nb, s, d = x.shape
    n_ctx = ctx.shape[1]
    depth = ada_w.shape[0]
    tm = TOKEN_TILE
    assert s % tm == 0 and nb * n_ctx == tm and s % GRID_W == 0 and nb + 1 <= MOD_ROWS
    dims = dict(batch=nb, seq=s, n_ctx=n_ctx, tiles_per_batch=s // tm, n_lat_tiles=nb * s // tm)

    cvec = jnp.concatenate([c, c_ctx[None, :], jnp.zeros((MOD_ROWS - nb - 1, d), F32)], axis=0)
    mod = _modulation(cvec, ada_w, ada_b)[:, :nb + 1].reshape(depth, nb + 1, 1, 6 * d)

    cs_nat, cs_t = _rope_tables(s, tm)
    head_of_lane = np.arange(LANES) // HEAD_DIM
    block_ones = jnp.asarray(head_of_lane[:, None] == head_of_lane[None, :], BF16)
    w_in_b = w_in.astype(BF16)
    w_out_b = w_out.astype(BF16)
    wg_b, wu_b, wd_b = w_gate.astype(BF16), w_up.astype(BF16), w_down.astype(BF16)
    rw_t = router_w.T.astype(BF16)
    rb_col = router_b.reshape(N_EXPERTS, 1).astype(F32)

    stream = (jnp.concatenate([x.reshape(nb * s, d), ctx.reshape(nb * n_ctx, d)], axis=0),)
    tq_g, tk_g = min(512, s), min(8192, s)
    tq_l = 256
    for l in range(depth):
        gq_cols = jnp.stack([qn_global[l], qn_local[l]]).reshape(2, HEAD_DIM, 1)
        gk_rows = jnp.stack([jnp.tile(kn_global[l], KV_HEADS), jnp.tile(kn_local[l], KV_HEADS)]).reshape(2, 1, LANES)
        outs = _pre_attention(
            stream, mod, l, attn_norm_g[l].reshape(1, d), w_in_b[l], gq_cols, gk_rows, block_ones, cs_nat, cs_t, dims)
        qgt, kg, vgt, qlt, kl, vlt = outs[:6]
        x_all = outs[6] if len(outs) > 6 else stream[0]
        ygt = _global_attention(qgt, kg, vgt, dims, tq_g, tk_g)
        ylt = _local_attention(sink_logit[l], qlt, kl, vlt, dims, tq_l)
        yct = _ctx_attention(sink_logit[l], qgt, kg, vgt, qlt, kl, vlt, dims)
        x1, hp, meta = _post_attention(ygt, ylt, yct, x_all, mod, l, w_out_b[l], ffn_norm_g[l].reshape(1, d),
                                        rw_t, rb_col, dims)
        stream = (x1, _moe(hp, meta, wg_b, wu_b, wd_b, l))
    return _combine(stream[0], stream[1], mod, depth - 1, nb * s, dims).reshape(nb, s, d)
```

```python
import functools
import math

import numpy as np
import jax
import jax.numpy as jnp
from jax import lax
from jax.experimental import pallas as pl
from jax.experimental.pallas import tpu as pltpu

F32 = jnp.float32
BF16 = jnp.bfloat16
FP8 = jnp.float8_e4m3fn

HEAD_DIM = 64
GRID_W = 64
WINDOW = 128
N_HEADS_MIXER = 8
KV_HEADS = 2
GROUP = N_HEADS_MIXER // KV_HEADS
ROPE_THETA = 10000.0
N_EXPERTS = 16
N_GROUPS = 4
EXPERTS_PER_GROUP = N_EXPERTS // N_GROUPS
EXPERT_PAIRS = tuple((a, b) for a in range(EXPERTS_PER_GROUP) for b in range(a + 1, EXPERTS_PER_GROUP))
N_CLASSES = N_GROUPS * len(EXPERT_PAIRS)
NORM_EPS = 1e-6
ATTN_SCALE = 1.0 / math.sqrt(HEAD_DIM)
LOG2_E = 1.4426950408889634
Q_SCALE = ATTN_SCALE * LOG2_E
NEG_INF = -1e30
N_FREQ = HEAD_DIM // 4

LANES = 128
TOKEN_TILE = 512
TOKEN_ROWS = 8
MOE_TILE = 256
QK_DEPTH = 256
R_ROW = 3 * HEAD_DIM
R_TILE = 32
R_MAX = 256.0
P_SHIFT = 3.0
P_LIMIT = 256.0
ONES_ROWS = 16
V_ROWS = HEAD_DIM + ONES_ROWS
VMEM_LIMIT = 48 * 1024 * 1024
MOD_ROWS = 16


def _dot(a, b):
    return jnp.dot(a, b, preferred_element_type=F32)


def _sigmoid(x):
    return 1.0 / (1.0 + jnp.exp(-x))


def _mod_kernel(c_ref, w_ref, b_ref, o_ref):
    c = c_ref[...]
    s = (c * _sigmoid(c)).astype(BF16)
    o_ref[0] = _dot(s, w_ref[0].astype(BF16)) + b_ref[0]


def _modulation(cvec, ada_w, ada_b):
    depth, d, d6 = ada_w.shape
    col = 1536
    assert d6 % col == 0
    return pl.pallas_call(
        _mod_kernel,
        grid=(depth, d6 // col),
        in_specs=[pl.BlockSpec((MOD_ROWS, d), lambda l, j: (0, 0)),
                  pl.BlockSpec((1, d, col), lambda l, j: (l, 0, j)),
                  pl.BlockSpec((1, 1, col), lambda l, j: (l, 0, j))],
        out_specs=pl.BlockSpec((1, MOD_ROWS, col), lambda l, j: (l, 0, j)),
        out_shape=jax.ShapeDtypeStruct((depth, MOD_ROWS, d6), F32),
        compiler_params=pltpu.CompilerParams(dimension_semantics=("arbitrary", "arbitrary"),
                                             vmem_limit_bytes=VMEM_LIMIT),
        name="modulation",
    )(cvec, ada_w, ada_b.reshape(depth, 1, d6))


def _rms_rows(x, gain):
    ms = jnp.mean(x * x, axis=-1, keepdims=True)
    return x * lax.rsqrt(ms + NORM_EPS) * gain


def _rope_t(t, cs_t):
    n = N_FREQ
    cr, sr, cc, sc = cs_t[0:n], cs_t[n:2 * n], cs_t[2 * n:3 * n], cs_t[3 * n:4 * n]
    x1r, x2r, x1c, x2c = t[0:n], t[n:2 * n], t[2 * n:3 * n], t[3 * n:4 * n]
    return jnp.concatenate([x1r * cr - x2r * sr, x2r * cr + x1r * sr,
                            x1c * cc - x2c * sc, x2c * cc + x1c * sc], axis=0)


def _q_heads_t(y_pair, gain_col, cs_t, q_scale):
    t = y_pair.T
    outs = []
    for j in range(2):
        th = t[j * HEAD_DIM:(j + 1) * HEAD_DIM]
        ms = jnp.mean(th * th, axis=0, keepdims=True)
        thn = th * lax.rsqrt(ms + NORM_EPS) * gain_col
        outs.append(_rope_t(thn, cs_t) * q_scale)
    return outs


def _split_fp8(x):
    hi = x.astype(FP8).astype(F32)
    lo = (x - hi).astype(FP8).astype(F32)
    return hi, lo


def _k_heads(kc, block_ones, gain_row, cs):
    sq = kc * kc
    hi = sq.astype(BF16)
    lo = (sq - hi.astype(F32)).astype(BF16)
    ss = _dot(hi, block_ones) + _dot(lo, block_ones)
    kn = kc * lax.rsqrt(ss * (1.0 / HEAD_DIM) + NORM_EPS) * gain_row
    lane = lax.broadcasted_iota(jnp.int32, kn.shape, 1)
    first_half = (lane % (2 * N_FREQ)) < N_FREQ
    swapped = jnp.where(first_half, pltpu.roll(kn, LANES - N_FREQ, 1), pltpu.roll(kn, N_FREQ, 1))
    return kn * cs[:, :LANES] + swapped * cs[:, LANES:]


def _pre_kernel(*refs, combine):
    if combine:
        x1_ref, yf_ref, modp_ref = refs[:3]
        refs = refs[3:]
        xo_ref = refs[-1]
        refs = refs[:-1]
        tm, d = x1_ref.shape
        x = x1_ref[...] + modp_ref[:, 5 * d:6 * d] * _from_token_tiles(yf_ref, tm, d // LANES)
        xo_ref[...] = x
    else:
        x = refs[0][...]
        refs = refs[1:]
    (mod_ref, g_ref, w_ref, gq_ref, gk_ref, bones_ref, cs_ref, cst_ref,
     qgt_ref, kg_ref, vgt_ref, qlt_ref, kl_ref, vlt_ref) = refs
    tm, d = x.shape
    shift = mod_ref[:, 0:d]
    scale = mod_ref[:, d:2 * d]
    h = _rms_rows(x, g_ref[...]) * (1.0 + scale) + shift
    y = _dot(h.astype(BF16), w_ref[...])
    cs_t = cst_ref[...]
    cs = cs_ref[...]
    zeros = jnp.zeros((HEAD_DIM, tm), BF16)
    qw = N_HEADS_MIXER * HEAD_DIM
    kw = KV_HEADS * HEAD_DIM
    for mixer, (qt_ref, k_ref, vt_ref) in enumerate(((qgt_ref, kg_ref, vgt_ref), (qlt_ref, kl_ref, vlt_ref))):
        base = mixer * (qw + 2 * kw)
        gain_col = gq_ref[mixer]
        split = mixer == 0
        for c in range(N_HEADS_MIXER // 2):
            pair = _q_heads_t(y[:, base + c * LANES: base + (c + 1) * LANES], gain_col, cs_t, Q_SCALE)
            for j in range(2):
                if split:
                    hi, lo = _split_fp8(pair[j])
                    qt_ref[2 * c + j] = jnp.concatenate([hi, hi, lo, jnp.zeros_like(hi)], axis=0).astype(FP8)
                else:
                    qt_ref[2 * c + j] = jnp.concatenate([pair[j].astype(BF16), zeros], axis=0)
        kc = y[:, base + qw: base + qw + kw]
        kboth = _k_heads(kc, bones_ref[...], gk_ref[mixer], cs)
        lane = lax.broadcasted_iota(jnp.int32, kboth.shape, 1)
        low = lane < HEAD_DIM
        ones_col = jnp.where(lane == HEAD_DIM, 1.0, 0.0)
        if split:
            hi, lo = _split_fp8(kboth)
            hi_sw, lo_sw = pltpu.roll(hi, HEAD_DIM, 1), pltpu.roll(lo, HEAD_DIM, 1)
            k_ref[0] = jnp.concatenate([jnp.where(low, hi, lo_sw), jnp.where(low, hi, ones_col)], axis=1).astype(FP8)
            k_ref[1] = jnp.concatenate([jnp.where(low, hi_sw, lo), jnp.where(low, hi_sw, ones_col)], axis=1).astype(FP8)
        else:
            k_ref[0] = jnp.where(low, kboth, ones_col).astype(BF16)
            k_ref[1] = jnp.where(low, pltpu.roll(kboth, HEAD_DIM, 1), ones_col).astype(BF16)
        vc = y[:, base + qw + kw: base + qw + 2 * kw]
        vt = vc.T.astype(BF16)
        for kv in range(KV_HEADS):
            vt_ref[kv] = vt[kv * HEAD_DIM:(kv + 1) * HEAD_DIM]


def _pre_attention(stream, mod, l, attn_g, w_in_b, gq_cols, gk_rows, block_ones, cs_nat, cs_t, dims):
    combine = len(stream) == 2
    ntok, d = stream[0].shape
    tm = TOKEN_TILE
    n_lat, per_b, nb = dims["n_lat_tiles"], dims["tiles_per_batch"], dims["batch"]
    n_tiles = ntok // tm

    def mod_idx(i):
        return (l, jnp.minimum(i // per_b, nb), 0, 0)

    def tab_idx(i):
        return jnp.where(i < n_lat, i % per_b, per_b)

    qt_shape = jax.ShapeDtypeStruct((N_HEADS_MIXER, LANES, ntok), BF16)
    k_shape = jax.ShapeDtypeStruct((KV_HEADS, ntok, LANES), BF16)
    vt_shape = jax.ShapeDtypeStruct((KV_HEADS, HEAD_DIM, ntok), BF16)
    qt_spec = pl.BlockSpec((N_HEADS_MIXER, LANES, tm), lambda i: (0, 0, i))
    k_spec = pl.BlockSpec((KV_HEADS, tm, LANES), lambda i: (0, i, 0))
    vt_spec = pl.BlockSpec((KV_HEADS, HEAD_DIM, tm), lambda i: (0, 0, i))
    qt8_shape = jax.ShapeDtypeStruct((N_HEADS_MIXER, QK_DEPTH, ntok), FP8)
    k8_shape = jax.ShapeDtypeStruct((KV_HEADS, ntok, QK_DEPTH), FP8)
    qt8_spec = pl.BlockSpec((N_HEADS_MIXER, QK_DEPTH, tm), lambda i: (0, 0, i))
    k8_spec = pl.BlockSpec((KV_HEADS, tm, QK_DEPTH), lambda i: (0, i, 0))
    row_spec =pl.BlockSpec((tm, d), lambda i: (i, 0))
    mod_spec = pl.BlockSpec((None, None, 1, mod.shape[-1]), mod_idx)
    if combine:
        modp_spec = pl.BlockSpec((None, None, 1, mod.shape[-1]), lambda i: (l - 1,) + mod_idx(i)[1:])
        tile_spec = pl.BlockSpec((tm * TOKEN_ROWS, LANES), lambda i: (i, 0))
        stream_specs, stream_args = [row_spec, tile_spec, modp_spec], [stream[0], stream[1], mod]
        extra_specs, extra_shapes = [row_spec], [jax.ShapeDtypeStruct((ntok, d), F32)]
    else:
        stream_specs, stream_args, extra_specs, extra_shapes = [row_spec], [stream[0]], [], []
    return pl.pallas_call(
        functools.partial(_pre_kernel, combine=combine),
        grid=(n_tiles,),
        in_specs=stream_specs + [
                  mod_spec,
                  pl.BlockSpec((1, d), lambda i: (0, 0)),
                  pl.BlockSpec(w_in_b.shape, lambda i: (0, 0)),
                  pl.BlockSpec(gq_cols.shape, lambda i: (0, 0, 0)),
                  pl.BlockSpec(gk_rows.shape, lambda i: (0, 0, 0)),
                  pl.BlockSpec(block_ones.shape, lambda i: (0, 0)),
                  pl.BlockSpec((tm, 2 * LANES), lambda i: (tab_idx(i), 0)),
                  pl.BlockSpec((HEAD_DIM, tm), lambda i: (0, tab_idx(i)))],
        out_specs=[qt8_spec, k8_spec, vt_spec, qt_spec, k_spec, vt_spec] + extra_specs,
        out_shape=[qt8_shape, k8_shape, vt_shape, qt_shape, k_shape, vt_shape] + extra_shapes,
        compiler_params=pltpu.CompilerParams(dimension_semantics=("arbitrary",), vmem_limit_bytes=VMEM_LIMIT),
        name="pre_attention",
    )(*stream_args, mod, attn_g, w_in_b, gq_cols, gk_rows, block_ones, cs_nat, cs_t)


def _with_ones(vt):
    return jnp.concatenate([vt, jnp.ones((ONES_ROWS, vt.shape[1]), BF16)], axis=0)


def _round_ref(x):
    inside = jnp.abs(x) <= R_MAX
    return jnp.where(inside, jnp.clip(x, -R_MAX, R_MAX).astype(FP8).astype(F32), x)


def _flash_kernel(q_ref, k_ref, v_ref, kc_ref, vc_ref, o_ref, r_scr, acc_scr, new_scr, pmax_scr):
    ki = pl.program_id(3)
    tq = q_ref.shape[2]
    zpad = jnp.zeros((QK_DEPTH - R_ROW - R_TILE, tq), FP8)

    def q_ext(g, r_row):
        neg = jnp.concatenate([-r_row, jnp.zeros((R_TILE - 1, tq), F32)], axis=0).astype(FP8)
        return jnp.concatenate([q_ref[g, 0:R_ROW], neg, zpad], axis=0)

    def two_pass_update(g, k, v1, first):
        r_old = jnp.zeros((1, tq), F32) if first else r_scr[g]
        st = _dot(k, q_ref[g]) - r_old
        top = jnp.max(st, axis=0, keepdims=True) - P_SHIFT
        delta = _round_ref(r_old + (top if first else jnp.maximum(top, 0.0))) - r_old
        p = jnp.exp2(st - delta).astype(BF16)
        pv = _dot(v1, p)
        acc_scr[g] = pv if first else acc_scr[g] * jnp.exp2(-delta) + pv
        r_scr[g] = r_old + delta

    @pl.when(ki == 0)
    def _():
        vc1 = _with_ones(vc_ref[...])
        kc = kc_ref[...]
        for g in range(GROUP):
            two_pass_update(g, kc, vc1, True)

    v1 = _with_ones(v_ref[...])
    v8 = v1.astype(FP8)
    k = k_ref[...]
    for g in range(GROUP):
        st = _dot(k, q_ext(g, r_scr[g]))
        p = jnp.exp2(st.astype(BF16))
        pmax_scr[g] = jnp.max(p, axis=0, keepdims=True).astype(F32)
        new_scr[g] = acc_scr[g] + _dot(v8, p.astype(FP8))
    clean = (jnp.max(pmax_scr[...]) <= P_LIMIT) & (jnp.max(jnp.abs(r_scr[...])) <= R_MAX)

    @pl.when(clean)
    def _():
        for g in range(GROUP):
            r_old = r_scr[g]
            r_new = _round_ref(r_old + jnp.maximum(jnp.log2(pmax_scr[g]) - P_SHIFT, 0.0))
            acc_scr[g] = new_scr[g] * jnp.exp2(r_old - r_new)
            r_scr[g] = r_new

    @pl.when(jnp.logical_not(clean))
    def _():
        for g in range(GROUP):
            two_pass_update(g, k, v1, False)

    @pl.when(ki == pl.num_programs(3) - 1)
    def _():
        for g in range(GROUP):
            a = acc_scr[g]
            o_ref[g * HEAD_DIM:(g + 1) * HEAD_DIM, :] = (a[0:HEAD_DIM] / a[HEAD_DIM:HEAD_DIM + 1]).astype(BF16)


def _global_attention(qt, k, vt, dims, tq, tk):
    nb, s, n_ctx = dims["batch"], dims["seq"], dims["n_ctx"]
    nq, nk = s // tq, s // tk
    ctx_blk0 = (nb * s) // n_ctx
    return pl.pallas_call(
        _flash_kernel,
        grid=(nb, KV_HEADS, nq, nk),
        in_specs=[pl.BlockSpec((GROUP, QK_DEPTH, tq), lambda b, kv, qi, ki: (kv, 0, b * nq + qi)),
                  pl.BlockSpec((None, tk, QK_DEPTH), lambda b, kv, qi, ki: (kv, b * nk + ki, 0)),
                  pl.BlockSpec((None, HEAD_DIM, tk), lambda b, kv, qi, ki: (kv, 0, b * nk + ki)),
                  pl.BlockSpec((None, n_ctx, QK_DEPTH), lambda b, kv, qi, ki: (kv, ctx_blk0 + b, 0)),
                  pl.BlockSpec((None, HEAD_DIM, n_ctx), lambda b, kv, qi, ki: (kv, 0, ctx_blk0 + b))],
        out_specs=pl.BlockSpec((GROUP * HEAD_DIM, tq), lambda b, kv, qi, ki: (kv, b * nq + qi)),
        out_shape=jax.ShapeDtypeStruct((N_HEADS_MIXER * HEAD_DIM, nb * s), BF16),
        scratch_shapes=[pltpu.VMEM((GROUP, 1, tq), F32), pltpu.VMEM((GROUP, V_ROWS, tq), F32),
                        pltpu.VMEM((GROUP, V_ROWS, tq), F32), pltpu.VMEM((GROUP, 1, tq), F32)],
        compiler_params=pltpu.CompilerParams(
            dimension_semantics=("arbitrary", "arbitrary", "arbitrary", "arbitrary"), vmem_limit_bytes=VMEM_LIMIT),
        name="global_attention",
    )(qt, k, vt, k, vt)


def _single_pass_attend(k_all, v1_all, q_t, bias, sink):
    st = _dot(k_all, q_t)
    if bias is not None:
        st = st + bias
    m = jnp.max(st, axis=0, keepdims=True)
    if sink is not None:
        m = jnp.maximum(m, sink)
    p = jnp.exp2((st - m).astype(BF16))
    a = _dot(v1_all, p)
    den = a[HEAD_DIM:HEAD_DIM + 1]
    if sink is not None:
        den = den + jnp.exp2(sink - m)
    return a[0:HEAD_DIM] / den


def _band_bias(tq, n_ctx):
    span = tq + 2 * WINDOW
    j = np.arange(span + n_ctx)[:, None]
    i = np.arange(tq)[None, :]
    band = (np.abs(j - WINDOW - i) <= WINDOW) | (j >= span)
    out = []
    for case in range(4):
        first, last = case & 1, case & 2
        ok = band & ~((j < WINDOW) & bool(first)) & ~((j >= WINDOW + tq) & (j < span) & bool(last))
        out.append(np.where(ok, 0.0, NEG_INF))
    return jnp.asarray(np.stack(out), F32)


def _local_kernel(sink_ref, q_ref, kp_ref, km_ref, kn_ref, kc_ref, vp_ref, vm_ref, vn_ref, vc_ref, bias_ref, o_ref):
    kv = pl.program_id(1)
    qi = pl.program_id(2)
    tq = q_ref.shape[2]
    case = jnp.where(qi == 0, 1, 0) + jnp.where(qi == pl.num_programs(2) - 1, 2, 0)
    bias = bias_ref[case]
    k_all = jnp.concatenate([kp_ref[...], km_ref[...], kn_ref[...], kc_ref[...]], axis=0)
    v1_all = _with_ones(jnp.concatenate([vp_ref[...], vm_ref[...], vn_ref[...], vc_ref[...]], axis=1))
    q_all = jnp.concatenate([q_ref[g] for g in range(GROUP)], axis=1)
    sink = jnp.concatenate([jnp.full((1, tq), sink_ref[kv * GROUP + g] * LOG2_E, F32) for g in range(GROUP)], axis=1)
    o = _single_pass_attend(k_all, v1_all, q_all, jnp.concatenate([bias] * GROUP, axis=1), sink)
    for g in range(GROUP):
        o_ref[g * HEAD_DIM:(g + 1) * HEAD_DIM, :] = o[:, g * tq:(g + 1) * tq].astype(BF16)


def _local_attention(sink, qt, k, vt, dims, tq):
    nb, s, n_ctx = dims["batch"], dims["seq"], dims["n_ctx"]
    nq = s // tq
    w = WINDOW
    per_b_w = s // w
    r = tq // w
    ctx_blk0 = (nb * s) // n_ctx
    bias = _band_bias(tq, n_ctx)

    def prev_idx(b, qi):
        return b * per_b_w + jnp.maximum(qi * r - 1, 0)

    def next_idx(b, qi):
        return b * per_b_w + jnp.minimum(qi * r + r, per_b_w - 1)

    return pl.pallas_call(
        _local_kernel,
        grid=(nb, KV_HEADS, nq),
        in_specs=[pl.BlockSpec(memory_space=pltpu.SMEM),
                  pl.BlockSpec((GROUP, LANES, tq), lambda b, kv, qi: (kv, 0, b * nq + qi)),
                  pl.BlockSpec((None, w, LANES), lambda b, kv, qi: (kv, prev_idx(b, qi), 0)),
                  pl.BlockSpec((None, tq, LANES), lambda b, kv, qi: (kv, b * nq + qi, 0)),
                  pl.BlockSpec((None, w, LANES), lambda b, kv, qi: (kv, next_idx(b, qi), 0)),
                  pl.BlockSpec((None, n_ctx, LANES), lambda b, kv, qi: (kv, ctx_blk0 + b, 0)),
                  pl.BlockSpec((None, HEAD_DIM, w), lambda b, kv, qi: (kv, 0, prev_idx(b, qi))),
                  pl.BlockSpec((None, HEAD_DIM, tq), lambda b, kv, qi: (kv, 0, b * nq + qi)),
                  pl.BlockSpec((None, HEAD_DIM, w), lambda b, kv, qi: (kv, 0, next_idx(b, qi))),
                  pl.BlockSpec((None, HEAD_DIM, n_ctx), lambda b, kv, qi: (kv, 0, ctx_blk0 + b)),
                  pl.BlockSpec(bias.shape, lambda b, kv, qi: (0, 0, 0))],
        out_specs=pl.BlockSpec((GROUP * HEAD_DIM, tq), lambda b, kv, qi: (kv, b * nq + qi)),
        out_shape=jax.ShapeDtypeStruct((N_HEADS_MIXER * HEAD_DIM, nb * s), BF16),
        compiler_params=pltpu.CompilerParams(
            dimension_semantics=("arbitrary", "arbitrary", "arbitrary"), vmem_limit_bytes=VMEM_LIMIT),
        name="local_attention",
    )(sink, qt, k, k, k, k, vt, vt, vt, vt, bias)


def _ctx_kernel(sink_ref, qg_ref, kg_ref, vg_ref, ql_ref, kl_ref, vl_ref, o_ref):
    tq = qg_ref.shape[2]
    for mixer, (q_ref, k_ref, v_ref) in enumerate(((qg_ref, kg_ref, vg_ref), (ql_ref, kl_ref, vl_ref))):
        for kv in range(KV_HEADS):
            k_all = k_ref[kv]
            v1 = _with_ones(v_ref[kv])
            for g in range(GROUP):
                hq = kv * GROUP + g
                sink = jnp.full((1, tq), sink_ref[hq] * LOG2_E, F32) if mixer == 1 else None
                o = _single_pass_attend(k_all, v1, q_ref[hq], None, sink)
                row = (mixer * N_HEADS_MIXER + hq) * HEAD_DIM
                o_ref[row:row + HEAD_DIM, :] = o.astype(BF16)


def _ctx_attention(sink, qgt, kg, vgt, qlt, kl, vlt, dims):
    nb, s, n_ctx = dims["batch"], dims["seq"], dims["n_ctx"]
    ctx_blk0 = (nb * s) // n_ctx
    def q_spec(depth):
        return pl.BlockSpec((N_HEADS_MIXER, depth, n_ctx), lambda b: (0, 0, ctx_blk0 + b))

    def k_spec(depth):
        return pl.BlockSpec((KV_HEADS, n_ctx, depth), lambda b: (0, ctx_blk0 + b, 0))

    v_spec = pl.BlockSpec((KV_HEADS, HEAD_DIM, n_ctx), lambda b: (0, 0, ctx_blk0 + b))
    rows = 2 * N_HEADS_MIXER * HEAD_DIM
    return pl.pallas_call(
        _ctx_kernel,
        grid=(nb,),
        in_specs=[pl.BlockSpec(memory_space=pltpu.SMEM), q_spec(QK_DEPTH), k_spec(QK_DEPTH), v_spec,
                  q_spec(LANES), k_spec(LANES), v_spec],
        out_specs=pl.BlockSpec((rows, n_ctx), lambda b: (0, b)),
        out_shape=jax.ShapeDtypeStruct((rows, nb * n_ctx), BF16),
        compiler_params=pltpu.CompilerParams(dimension_semantics=("arbitrary",), vmem_limit_bytes=VMEM_LIMIT),
        name="ctx_attention",
    )(sink, qgt, kg, vgt, qlt, kl, vlt)


def _to_token_tiles(ref, mat):
    t = mat.shape[0]
    for j in range(mat.shape[1] // LANES):
        ref[pl.ds(j, t, stride=TOKEN_ROWS), :] = mat[:, j * LANES:(j + 1) * LANES]


def _from_token_tiles(ref, t, n_chunks):
    return jnp.concatenate([ref[pl.ds(j, t, stride=TOKEN_ROWS), :] for j in range(n_chunks)], axis=1)


def _route(sel, scores):
    n = EXPERTS_PER_GROUP
    group_score = []
    for g in range(N_GROUPS):
        r = sel[g * n:(g + 1) * n]
        best = None
        for a in range(n):
            for b in range(a + 1, n):
                pair = r[a] + r[b]
                best = pair if best is None else jnp.maximum(best, pair)
        group_score.append(best)
    gates = [None] * N_EXPERTS
    picked = []
    for g in range(N_GROUPS):
        chosen = None
        for h in range(N_GROUPS):
            if h == g:
                continue
            c = (group_score[g] > group_score[h]) if h < g else (group_score[g] >= group_score[h])
            chosen = c if chosen is None else (chosen & c)
        for a in range(n):
            ea = g * n + a
            rank = jnp.zeros_like(sel[ea])
            for b in range(n):
                if b == a:
                    continue
                eb = g * n + b
                ahead = (sel[eb] >= sel[ea]) if b < a else (sel[eb] > sel[ea])
                rank = rank + jnp.where(ahead, 1.0, 0.0)
            picked.append(chosen & (rank < 2.0))
    den = None
    for e in range(N_EXPERTS):
        w = jnp.where(picked[e], scores[e], 0.0)
        gates[e] = w
        den = w if den is None else den + w
    cls = jnp.zeros_like(den)
    for g in range(N_GROUPS):
        for code, (a, b) in enumerate(EXPERT_PAIRS):
            both = picked[g * n + a] & picked[g * n + b]
            cls = cls + jnp.where(both, float(g * len(EXPERT_PAIRS) + code), 0.0)
    return [w / den for w in gates], cls


def _post_kernel(yg_ref, yl_ref, yc_ref, x_ref, mod_ref, wout_ref, g_ref, rwt_ref, rb_ref,
                 x1_ref, hp_ref, meta_ref, *, n_lat):
    i = pl.program_id(0)
    d = x_ref.shape[1]
    tm = x_ref.shape[0]
    yt_lat = jnp.concatenate([yg_ref[...], yl_ref[...]], axis=0)
    is_lat = jnp.full(yt_lat.shape, i, jnp.int32) < n_lat
    yt = jnp.where(is_lat, yt_lat, yc_ref[...])
    a = lax.dot_general(yt, wout_ref[...], (((0,), (0,)), ((), ())), preferred_element_type=F32)
    gate1 = mod_ref[:, 2 * d:3 * d]
    shift2 = mod_ref[:, 3 * d:4 * d]
    scale2 = mod_ref[:, 4 * d:5 * d]
    x1 = x_ref[...] + gate1 * a
    x1_ref[...] = x1
    h2 = _rms_rows(x1, g_ref[...]) * (1.0 + scale2) + shift2
    logits_t = lax.dot_general(rwt_ref[...], h2.astype(BF16), (((1,), (1,)), ((), ())),
                               preferred_element_type=F32)
    scores_t = _sigmoid(logits_t)
    sel_t = scores_t + rb_ref[...]
    gates, cls = _route([sel_t[e:e + 1] for e in range(N_EXPERTS)], [scores_t[e:e + 1] for e in range(N_EXPERTS)])
    gates_t = jnp.concatenate(gates + [jnp.zeros((LANES - N_EXPERTS, tm), F32)], axis=0)
    bits = lax.bitcast_convert_type(h2.astype(BF16).astype(F32), jnp.uint32)
    half = d // 2
    words = (bits[:, :half] & jnp.uint32(0xFFFF0000)) | (bits[:, half:] >> 16)
    spare = jnp.zeros((tm, TOKEN_ROWS * LANES - half - LANES), jnp.uint32)
    _to_token_tiles(hp_ref, jnp.concatenate([words, lax.bitcast_convert_type(gates_t.T, jnp.uint32), spare], axis=1))
    meta_ref[...] = jnp.concatenate([cls, jnp.zeros((7, tm), F32)], axis=0)


def _post_attention(ygt, ylt, yct, x_all, mod, l, w_out_b, ffn_g, rw_t, rb_col, dims):
    ntok, d = x_all.shape
    tm = TOKEN_TILE
    n_lat, per_b, nb = dims["n_lat_tiles"], dims["tiles_per_batch"], dims["batch"]
    rows = ygt.shape[0]

    def mod_idx(i):
        return (l, jnp.minimum(i // per_b, nb), 0, 0)

    def lat_idx(i):
        return (0, jnp.minimum(i, n_lat - 1))

    return pl.pallas_call(
        functools.partial(_post_kernel, n_lat=n_lat),
        grid=(ntok // tm,),
        in_specs=[pl.BlockSpec((rows, tm), lat_idx),
                  pl.BlockSpec((rows, tm), lat_idx),
                  pl.BlockSpec((2 * rows, tm), lambda i: (0, 0)),
                  pl.BlockSpec((tm, d), lambda i: (i, 0)),
                  pl.BlockSpec((None, None, 1, mod.shape[-1]), mod_idx),
                  pl.BlockSpec(w_out_b.shape, lambda i: (0, 0)),
                  pl.BlockSpec((1, d), lambda i: (0, 0)),
                  pl.BlockSpec(rw_t.shape, lambda i: (0, 0)),
                  pl.BlockSpec(rb_col.shape, lambda i: (0, 0))],
        out_specs=[pl.BlockSpec((tm, d), lambda i: (i, 0)),
                   pl.BlockSpec((tm * TOKEN_ROWS, LANES), lambda i: (i, 0)),
                   pl.BlockSpec((8, tm), lambda i: (0, i))],
        out_shape=[jax.ShapeDtypeStruct((ntok, d), F32),
                   jax.ShapeDtypeStruct((ntok * TOKEN_ROWS, LANES), jnp.uint32),
                   jax.ShapeDtypeStruct((8, ntok), F32)],
        compiler_params=pltpu.CompilerParams(dimension_semantics=("arbitrary",), vmem_limit_bytes=VMEM_LIMIT),
        name="post_attention",
    )(ygt, ylt, yct, x_all, mod, w_out_b, ffn_g, rw_t, rb_col)


def _sort_plan(cls, tm):
    ntok = cls.shape[0]
    p_tiles = -(-ntok // tm) + N_CLASSES
    order = jnp.argsort(cls, stable=True).astype(jnp.int32)
    classes = jnp.arange(N_CLASSES, dtype=jnp.int32)
    counts = jnp.sum((cls[:, None] == classes[None, :]).astype(jnp.int32), axis=0)
    tiles_per = (counts + tm - 1) // tm
    tile_end = jnp.cumsum(tiles_per)
    start = jnp.cumsum(counts) - counts
    tile = jnp.arange(p_tiles, dtype=jnp.int32)
    tile_cls = jnp.minimum(jnp.sum((tile_end[None, :] <= tile[:, None]).astype(jnp.int32), axis=1), N_CLASSES - 1)
    is_cls = tile_cls[:, None] == classes[None, :]

    def per_tile(v):
        return jnp.sum(jnp.where(is_cls, v[None, :], 0), axis=1)

    row0 = (tile - per_tile(tile_end - tiles_per)) * tm
    left = jnp.where(tile < tile_end[N_CLASSES - 1], per_tile(counts) - row0, 0)
    p = p_tiles * tm
    pad_before = (tile_end - tiles_per) * tm - start
    base = jnp.concatenate([jnp.zeros((p,), jnp.int32), order, jnp.zeros((p,), jnp.int32)])
    token = jnp.zeros((p_tiles, tm), jnp.int32)
    for c in range(N_CLASSES):
        shifted = lax.dynamic_slice(base, (p - pad_before[c],), (p,)).reshape(p_tiles, tm)
        token = jnp.where(is_cls[:, c:c + 1], shifted, token)
    r = jnp.arange(tm, dtype=jnp.int32)[None, :]
    valid = r < left[:, None]
    src = jnp.where(valid, token, 0).astype(jnp.int32)
    dst = jnp.where(valid, token, ntok + (tile % 2)[:, None] * tm + r).astype(jnp.int32)
    pair = jnp.asarray(EXPERT_PAIRS, jnp.int32)[tile_cls % len(EXPERT_PAIRS)]
    base_e = (tile_cls // len(EXPERT_PAIRS)) * EXPERTS_PER_GROUP
    return dict(src=src.reshape(p_tiles, 1, tm), dst=dst.reshape(p_tiles, 1, tm), ea=base_e + pair[:, 0],
                eb=base_e + pair[:, 1], n_used=tile_end[N_CLASSES - 1:].astype(jnp.int32), p_tiles=p_tiles)


ROW_UNROLL = 8


def _row_copy_all(idx_ref, n_rows, make_copy):
    def body(i, carry):
        for j in range(ROW_UNROLL):
            r = i * ROW_UNROLL + j
            make_copy(idx_ref[0, 0, r], r).start(priority=j % 2)
        return carry
    lax.fori_loop(0, n_rows // ROW_UNROLL, body, 0)


def _token_tile(ref, index):
    return ref.at[pl.ds(pl.multiple_of(index * TOKEN_ROWS, TOKEN_ROWS), TOKEN_ROWS)]


def _expert_pair_kernel(ea_ref, eb_ref, nu_ref, src_ref, nxt_ref, dst_ref, hp_hbm,
                        wga_ref, wua_ref, wda_ref, wgb_ref, wub_ref, wdb_ref, yf_hbm, hbuf, ybuf, gsem, ssem):
    t = pl.program_id(0)
    n_used = nu_ref[0]
    tm = src_ref.shape[2]
    d = wda_ref.shape[3]
    slot = t % 2
    other = 1 - slot

    def gather(idx_ref, s):
        _row_copy_all(idx_ref, tm, lambda tok, r: pltpu.make_async_copy(
            _token_tile(hp_hbm, tok), _token_tile(hbuf.at[s], r), gsem.at[s]))

    def gather_done(s):
        pltpu.make_async_copy(hp_hbm.at[pl.ds(0, tm * TOKEN_ROWS)], hbuf.at[s], gsem.at[s]).wait()

    def scatter_done(s):
        pltpu.make_async_copy(ybuf.at[s], yf_hbm.at[pl.ds(0, tm * TOKEN_ROWS)], ssem.at[s]).wait()

    @pl.when(t < n_used)
    def _():
        @pl.when(t == 0)
        def _():
            gather(src_ref, 0)
            ybuf[...] = jnp.zeros_like(ybuf)
            n_tok = yf_hbm.shape[0] // TOKEN_ROWS - 2 * tm
            for s in range(2):
                fill = pltpu.make_async_copy(
                    ybuf.at[s], yf_hbm.at[pl.ds((n_tok + s * tm) * TOKEN_ROWS, tm * TOKEN_ROWS)], ssem.at[s])
                fill.start()
                fill.wait()

        @pl.when(t + 1 < n_used)
        def _():
            gather(nxt_ref, other)

        gather_done(slot)

        @pl.when(t >= 2)
        def _():
            scatter_done(slot)

        tiles = hbuf.at[slot]
        half_chunks = d // (2 * LANES)
        words = _from_token_tiles(tiles, tm, half_chunks)
        h = jnp.concatenate([lax.bitcast_convert_type(words & jnp.uint32(0xFFFF0000), F32).astype(BF16),
                             lax.bitcast_convert_type(words << 16, F32).astype(BF16)], axis=1)
        gates = lax.bitcast_convert_type(tiles[pl.ds(half_chunks, tm, stride=TOKEN_ROWS), :], F32)
        lane = lax.broadcasted_iota(jnp.int32, gates.shape, 1)
        y = None
        for e_ref, wg_ref, wu_ref, wd_ref in ((ea_ref, wga_ref, wua_ref, wda_ref), (eb_ref, wgb_ref, wub_ref, wdb_ref)):
            gcol = jnp.sum(jnp.where(lane == e_ref[t], gates, 0.0), axis=1, keepdims=True)
            a = _dot(h, wg_ref[0, 0])
            u = _dot(h, wu_ref[0, 0])
            he = ((a * _sigmoid(a)) * u * gcol).astype(BF16)
            part = _dot(he, wd_ref[0, 0])
            y = part if y is None else y + part
        _to_token_tiles(ybuf.at[slot], y)
        _row_copy_all(dst_ref, tm, lambda tok, r: pltpu.make_async_copy(
            _token_tile(ybuf.at[slot], r), _token_tile(yf_hbm, tok), ssem.at[slot]))

        @pl.when(t == n_used - 1)
        def _():
            @pl.when(t >= 1)
            def _():
                scatter_done(other)

            scatter_done(slot)


def _moe(hp, meta, wg_b, wu_b, wd_b, l):
    ntok = hp.shape[0] // TOKEN_ROWS
    tm = MOE_TILE
    d = wd_b.shape[-1]
    f = wg_b.shape[-1]
    plan = _sort_plan(meta[0].astype(jnp.int32), tm)
    p_tiles = plan["p_tiles"]

    def w_spec(shape, which):
        return pl.BlockSpec((1, 1) + shape, lambda t, ea, eb, nu: (l, (ea, eb)[which][t], 0, 0))

    def idx_spec(ahead):
        return pl.BlockSpec((1, 1, tm), lambda t, ea, eb, nu: (jnp.minimum(t + ahead, p_tiles - 1), 0, 0),
                            memory_space=pltpu.SMEM)

    return pl.pallas_call(
        _expert_pair_kernel,
        grid_spec=pltpu.PrefetchScalarGridSpec(
            num_scalar_prefetch=3,
            grid=(p_tiles,),
            in_specs=[idx_spec(0), idx_spec(1), idx_spec(0), pl.BlockSpec(memory_space=pl.ANY),
                      w_spec((d, f), 0), w_spec((d, f), 0), w_spec((f, d), 0),
                      w_spec((d, f), 1), w_spec((d, f), 1), w_spec((f, d), 1)],
            out_specs=pl.BlockSpec(memory_space=pl.ANY),
            scratch_shapes=[pltpu.VMEM((2, tm * TOKEN_ROWS, LANES), jnp.uint32),
                            pltpu.VMEM((2, tm * TOKEN_ROWS, LANES), F32),
                            pltpu.SemaphoreType.DMA((2,)), pltpu.SemaphoreType.DMA((2,))]),
        out_shape=jax.ShapeDtypeStruct(((ntok + 2 * tm) * TOKEN_ROWS, LANES), F32),
        compiler_params=pltpu.CompilerParams(dimension_semantics=("arbitrary",), vmem_limit_bytes=VMEM_LIMIT),
        name="moe_expert_pairs",
    )(plan["ea"], plan["eb"], plan["n_used"], plan["src"], plan["src"], plan["dst"], hp,
      wg_b, wu_b, wd_b, wg_b, wu_b, wd_b)


def _combine_kernel(x1_ref, yf_ref, mod_ref, o_ref):
    tm, d = x1_ref.shape
    o_ref[...] = x1_ref[...] + mod_ref[:, 5 * d:6 * d] * _from_token_tiles(yf_ref, tm, d // LANES)


def _combine(x1, yf, mod, l, n_rows, dims):
    d = x1.shape[1]
    tm = TOKEN_TILE
    per_b, nb = dims["tiles_per_batch"], dims["batch"]
    return pl.pallas_call(
        _combine_kernel,
        grid=(n_rows // tm,),
        in_specs=[pl.BlockSpec((tm, d), lambda i: (i, 0)),
                  pl.BlockSpec((tm * TOKEN_ROWS, LANES), lambda i: (i, 0)),
                  pl.BlockSpec((None, None, 1, mod.shape[-1]), lambda i: (l, jnp.minimum(i // per_b, nb), 0, 0))],
        out_specs=pl.BlockSpec((tm, d), lambda i: (i, 0)),
        out_shape=jax.ShapeDtypeStruct((n_rows, d), F32),
        compiler_params=pltpu.CompilerParams(dimension_semantics=("arbitrary",), vmem_limit_bytes=VMEM_LIMIT),
        name="moe_combine",
    )(x1, yf, mod)


def _rope_tables(s, pad):
    rows = s // GRID_W
    row = jnp.repeat(jnp.arange(rows, dtype=jnp.int32), GRID_W).astype(F32)
    col = jnp.tile(jnp.arange(GRID_W, dtype=jnp.int32), rows).astype(F32)
    inv = ROPE_THETA ** (-jnp.arange(N_FREQ, dtype=F32) / N_FREQ)
    ang_r, ang_c = row[:, None] * inv, col[:, None] * inv
    cr, sr, cc, sc = jnp.cos(ang_r), jnp.sin(ang_r), jnp.cos(ang_c), jnp.sin(ang_c)
    one, zero = jnp.ones((pad, N_FREQ), F32), jnp.zeros((pad, N_FREQ), F32)
    cs_t = jnp.concatenate([jnp.concatenate([cr, sr, cc, sc], axis=1),
                            jnp.concatenate([one, zero, one, zero], axis=1)], axis=0).T
    cpat = jnp.concatenate([cr, cr, cc, cc], axis=1)
    spat = jnp.concatenate([-sr, sr, -sc, sc], axis=1)
    ident = jnp.concatenate([jnp.ones((pad, 2 * HEAD_DIM), F32), jnp.zeros((pad, 2 * HEAD_DIM), F32)], axis=1)
    cs_nat = jnp.concatenate([jnp.concatenate([cpat, cpat, spat, spat], axis=1), ident], axis=0)
    return cs_nat, cs_t


def kernel(x, c, ctx, c_ctx, ada_w, ada_b, attn_norm_g, ffn_norm_g, w_in, qn_global, kn_global, qn_local, kn_local,
           sink_logit, w_out, router_w, router_b, w_gate, w_up, w_down):
    nb, s, d = x.shape
    n_ctx = ctx.shape[1]
    depth = ada_w.shape[0]
    tm = TOKEN_TILE
    assert s % tm == 0 and nb * n_ctx == tm and s % GRID_W == 0 and nb + 1 <= MOD_ROWS
    dims = dict(batch=nb, seq=s, n_ctx=n_ctx, tiles_per_batch=s // tm, n_lat_tiles=nb * s // tm)

    cvec = jnp.concatenate([c, c_ctx[None, :], jnp.zeros((MOD_ROWS - nb - 1, d), F32)], axis=0)
    mod = _modulation(cvec, ada_w, ada_b)[:, :nb + 1].reshape(depth, nb + 1, 1, 6 * d)

    cs_nat, cs_t = _rope_tables(s, tm)
    head_of_lane = np.arange(LANES) // HEAD_DIM
    block_ones = jnp.asarray(head_of_lane[:, None] == head_of_lane[None, :], BF16)
    w_in_b = w_in.astype(BF16)
    w_out_b = w_out.astype(BF16)
    wg_b, wu_b, wd_b = w_gate.astype(BF16), w_up.astype(BF16), w_down.astype(BF16)
    rw_t = router_w.T.astype(BF16)
    rb_col = router_b.reshape(N_EXPERTS, 1).astype(F32)

    stream = (jnp.concatenate([x.reshape(nb * s, d), ctx.reshape(nb * n_ctx, d)], axis=0),)
    tq_g, tk_g = min(512, s), min(8192, s)
    tq_l = 256
    for l in range(depth):
        gq_cols = jnp.stack([qn_global[l], qn_local[l]]).reshape(2, HEAD_DIM, 1)
        gk_rows = jnp.stack([jnp.tile(kn_global[l], KV_HEADS), jnp.tile(kn_local[l], KV_HEADS)]).reshape(2, 1, LANES)
        outs = _pre_attention(
            stream, mod, l, attn_norm_g[l].reshape(1, d), w_in_b[l], gq_cols, gk_rows, block_ones, cs_nat, cs_t, dims)
        qgt, kg, vgt, qlt, kl, vlt = outs[:6]
        x_all = outs[6] if len(outs) > 6 else stream[0]
        ygt = _global_attention(qgt, kg, vgt, dims, tq_g, tk_g)
        ylt = _local_attention(sink_logit[l], qlt, kl, vlt, dims, tq_l)
        yct = _ctx_attention(sink_logit[l], qgt, kg, vgt, qlt, kl, vlt, dims)
        x1, hp, meta = _post_attention(ygt, ylt, yct, x_all, mod, l, w_out_b[l], ffn_norm_g[l].reshape(1, d),
                                        rw_t, rb_col, dims)
        stream = (x1, _moe(hp, meta, wg_b, wu_b, wd_b, l))
    return _combine(stream[0], stream[1], mod, depth - 1, nb * s, dims).reshape(nb, s, d)
```
